```python
import jax, jax.numpy as jnp
from jax import lax
import numpy as np

D_MODEL = 2048
BATCH = 2
SEQ = 4096
DEPTH = 2
DEC_BATCH = 8
DEC_SEQ = 4
PAST_LEN = 16384
PAGE_SIZE = 128

HEAD_DIM = 128
H_M = D_MODEL // 4 // HEAD_DIM
DK_M = HEAD_DIM
DV_M = HEAD_DIM
H_G = D_MODEL // 4 // HEAD_DIM
DK_G = HEAD_DIM // 2
DV_G = HEAD_DIM
GLA_RANK = 16
GLA_TAU = 16.0
H_F = D_MODEL // 2 // HEAD_DIM
DH_F = HEAD_DIM
MIX_WIDTH = H_M * DV_M + H_G * DV_G + H_F * DH_F
D_FF = ((8 * D_MODEL // 3 + 127) // 128) * 128
CONV_W = 3
CHUNK = 64
Q_BLOCK = 128
ALPHA = (2.0 * DEPTH) ** 0.25
BETA = (8.0 * DEPTH) ** -0.25
LN_EPS = 1e-5
NORM_EPS = 1e-6
FOX_FORGET_BIAS = 6.0

kernel_name = "hymba_mlstm_gla_fox_decoder_step"

F32 = jnp.float32


def _proj_layout():
    return (("m_q", H_M * DK_M), ("m_k", H_M * DK_M), ("m_v", H_M * DV_M), ("m_o", H_M * DV_M),
            ("m_i", H_M), ("m_f", H_M),
            ("g_q", H_G * DK_G), ("g_k", H_G * DK_G), ("g_v", H_G * DV_G), ("g_r", H_G * DV_G),
            ("g_a", GLA_RANK),
            ("f_q", H_F * DH_F), ("f_k", H_F * DH_F), ("f_v", H_F * DH_F), ("f_f", H_F))


def _layer_norm(x, g, b):
    mu = jnp.mean(x, axis=-1, keepdims=True)
    var = jnp.mean(jnp.square(x - mu), axis=-1, keepdims=True)
    return (x - mu) * lax.rsqrt(var + LN_EPS) * g + b


def _head_rms(h, n_heads, g):
    B, T, W = h.shape
    hh = h.reshape(B, T, n_heads, W // n_heads)
    hh = hh * lax.rsqrt(jnp.mean(jnp.square(hh), axis=-1, keepdims=True) + NORM_EPS)
    return hh.reshape(B, T, W) * g


def _heads(a, n_heads):
    B, T, W = a.shape
    return a.reshape(B, T, n_heads, W // n_heads).transpose(0, 2, 1, 3)


def _merge(a):
    B, H, T, d = a.shape
    return a.transpose(0, 2, 1, 3).reshape(B, T, H * d)


def _chunk_len(T):
    return CHUNK if T % CHUNK == 0 else T


def _chunks(a, L):
    B, H, T = a.shape[:3]
    return jnp.moveaxis(a.reshape(B, H, T // L, L, *a.shape[3:]), 2, 0)


def _unchunk(a):
    a = jnp.moveaxis(a, 0, 2)
    return a.reshape(a.shape[0], a.shape[1], -1, *a.shape[4:])


def _mlstm_chunked(q, k, v, ig, lf, c0, n0, m0):
    L = _chunk_len(q.shape[2])
    causal = jnp.tril(jnp.ones((L, L), bool))

    def step(carry, inp):
        c, n, m = carry
        qc, kc, vc, ic, fc = inp
        b = jnp.cumsum(fc, axis=-1)
        log_d = jnp.where(causal, b[..., :, None] - b[..., None, :] + ic[..., None, :], -jnp.inf)
        m_inter = b + m[..., None]
        m_t = jnp.maximum(jnp.max(log_d, axis=-1), m_inter)
        w = jnp.einsum("bhtd,bhsd->bhts", qc, kc) * jnp.exp(log_d - m_t[..., None])
        s_inter = jnp.exp(m_inter - m_t)
        num = jnp.einsum("bhts,bhsv->bhtv", w, vc) + s_inter[..., None] * jnp.einsum("bhvd,bhtd->bhtv", c, qc)
        den = jnp.sum(w, axis=-1) + s_inter * jnp.einsum("bhd,bhtd->bht", n, qc)
        h = num / jnp.maximum(jnp.abs(den), jnp.exp(-m_t))[..., None]
        b_last = b[..., -1]
        log_e = b_last[..., None] - b + ic
        m_new = jnp.maximum(b_last + m, jnp.max(log_e, axis=-1))
        e = jnp.exp(log_e - m_new[..., None])
        decay = jnp.exp(b_last + m - m_new)
        c_new = decay[..., None, None] * c + jnp.einsum("bhs,bhsv,bhsd->bhvd", e, vc, kc)
        n_new = decay[..., None] * n + jnp.einsum("bhs,bhsd->bhd", e, kc)
        return (c_new, n_new, m_new), h

    (c, n, m), hs = lax.scan(step, (c0, n0, m0), tuple(_chunks(a, L) for a in (q, k, v, ig, lf)))
    return _unchunk(hs), c, n, m


def _gla_chunked(q, k, v, la, s0):
    L = _chunk_len(q.shape[2])
    causal = jnp.tril(jnp.ones((L, L), bool))[..., None]

    def step(s, inp):
        qc, kc, vc, ac = inp
        g = jnp.cumsum(ac, axis=2)
        rel = jnp.where(causal, g[:, :, :, None, :] - g[:, :, None, :, :], -jnp.inf)
        a = jnp.einsum("bhtd,bhsd,bhtsd->bhts", qc, kc, jnp.exp(rel))
        o = jnp.einsum("bhts,bhsv->bhtv", a, vc) + jnp.einsum("bhtd,bhdv->bhtv", qc * jnp.exp(g), s)
        g_last = g[:, :, -1]
        s_new = jnp.exp(g_last)[..., None] * s + jnp.einsum(
            "bhsd,bhsv->bhdv", kc * jnp.exp(g_last[:, :, None, :] - g), vc)
        return s_new, o

    s, os_ = lax.scan(step, s0, tuple(_chunks(a, L) for a in (q, k, v, la)))
    return _unchunk(os_), s


def _fox_prompt(q, k, v, lf):
    B, H, T, D = q.shape
    qb_len = Q_BLOCK if T % Q_BLOCK == 0 else T
    nb = T // qb_len
    c = jnp.cumsum(lf, axis=-1)
    kpos = jnp.arange(T)
    scale = D ** -0.5

    def block(args):
        qi, ci, start = args
        s = jnp.einsum("bhqd,bhkd->bhqk", qi, k) * scale + ci[..., :, None] - c[..., None, :]
        qpos = start + jnp.arange(qb_len)
        s = jnp.where(qpos[:, None] >= kpos[None, :], s, -jnp.inf)
        return jnp.einsum("bhqk,bhkd->bhqd", jax.nn.softmax(s, axis=-1), v)

    qb = jnp.moveaxis(q.reshape(B, H, nb, qb_len, D), 2, 0)
    cb = jnp.moveaxis(c.reshape(B, H, nb, qb_len), 2, 0)
    out = lax.map(block, (qb, cb, jnp.arange(nb) * qb_len))
    return jnp.moveaxis(out, 0, 2).reshape(B, H, T, D)


def _fox_sample(q, k, v, lf, k_past, v_past, lf_past):
    S, D = q.shape[2], q.shape[3]
    P = k_past.shape[1]
    scale = D ** -0.5
    c_new = jnp.cumsum(lf, axis=-1)
    lfp = lf_past.transpose(0, 2, 1)
    suffix = lax.cumsum(lfp, axis=2, reverse=True) - lfp
    s_past = jnp.einsum("bhqd,bkhd->bhqk", q, k_past) * scale + c_new[..., :, None] + suffix[..., None, :]
    s_self = jnp.einsum("bhqd,bhkd->bhqk", q, k) * scale + c_new[..., :, None] - c_new[..., None, :]
    s_self = jnp.where(jnp.tril(jnp.ones((S, S), bool)), s_self, -jnp.inf)
    p = jax.nn.softmax(jnp.concatenate([s_past, s_self], axis=-1), axis=-1)
    return (jnp.einsum("bhqk,bkhd->bhqd", p[..., :P], v_past)
            + jnp.einsum("bhqk,bhkd->bhqd", p[..., P:], v))


def _token_mixer(x, mix_params, mlstm_state, gla_state, past):
    (w_in, b_m_ig, b_m_fg, w_g_alpha_up, b_g_alpha, b_f, g_m_norm, g_g_norm, w_out) = mix_params
    B, T, _ = x.shape
    z = jnp.matmul(x, w_in).astype(F32)
    idx = np.cumsum([s for _, s in _proj_layout()])[:-1].tolist()
    (mq, mk, mv, mo, mi, mf, gq, gk, gv, gr, ga, fq, fk, fv, ff) = jnp.split(z, idx, axis=-1)

    ig = (mi + b_m_ig).transpose(0, 2, 1)
    lf_m = jax.nn.log_sigmoid(mf + b_m_fg).transpose(0, 2, 1)
    hm, c_new, n_new, m_new = _mlstm_chunked(
        _heads(mq, H_M), _heads(mk, H_M) * DK_M ** -0.5, _heads(mv, H_M), ig, lf_m, *mlstm_state)
    hm = _head_rms(jax.nn.sigmoid(mo) * _merge(hm), H_M, g_m_norm)

    la = jax.nn.log_sigmoid(jnp.matmul(ga, w_g_alpha_up) + b_g_alpha) / GLA_TAU
    hg, s_new = _gla_chunked(_heads(gq, H_G) * DK_G ** -0.5, _heads(gk, H_G), _heads(gv, H_G),
                             _heads(la, H_G), gla_state)
    hg = jax.nn.silu(gr) * _head_rms(_merge(hg), H_G, g_g_norm)

    k_rows = fk.reshape(B, T, H_F, DH_F)
    v_rows = fv.reshape(B, T, H_F, DH_F)
    lf_rows = jax.nn.log_sigmoid(ff + b_f)
    qf = _heads(fq, H_F)
    kf = k_rows.transpose(0, 2, 1, 3)
    vf = v_rows.transpose(0, 2, 1, 3)
    lff = lf_rows.transpose(0, 2, 1)
    if past is None:
        hf = _fox_prompt(qf, kf, vf, lff)
    else:
        hf = _fox_sample(qf, kf, vf, lff, *past)
    hf = _merge(hf)

    out = jnp.matmul(jnp.concatenate([hm, hg, hf], axis=-1), w_out).astype(F32)
    return out, (c_new, n_new, m_new, s_new, k_rows, v_rows, lf_rows)


def _conv_ffn(h, w_up, conv_w, conv_b, w_down, prev):
    u = jnp.matmul(h, w_up).astype(F32)
    T = u.shape[1]
    up = jnp.concatenate([prev.astype(F32), u], axis=1)
    y = conv_b + sum(conv_w[i] * up[:, i:i + T] for i in range(CONV_W))
    gate, val = jnp.split(y, 2, axis=-1)
    out = jnp.matmul(jax.nn.silu(gate) * val, w_down).astype(F32)
    return out, up[:, -(CONV_W - 1):]


def _layer(x, mix_params, ffn_params, norm_params, states, past):
    ln1_g, ln1_b, ln2_g, ln2_b = norm_params
    mc, mn, mm, gs, conv_prev = states
    mix, (c, n, m, s, k_rows, v_rows, lf_rows) = _token_mixer(x, mix_params, (mc, mn, mm), gs, past)
    h = _layer_norm(ALPHA * x + mix, ln1_g, ln1_b)
    f, conv_new = _conv_ffn(h, *ffn_params, conv_prev)
    x = _layer_norm(ALPHA * h + f, ln2_g, ln2_b)
    return x, (c, n, m, s, conv_new, k_rows, v_rows, lf_rows)


def _gather_pages(pool, page_table):
    g = pool[page_table]
    return g.reshape(page_table.shape[0], -1, *pool.shape[2:]).astype(F32)


def setup_inputs(seed: int = 0) -> dict:
    key = jax.random.key(seed)
    ks = jax.random.split(key, 32)
    n_pages = PAST_LEN // PAGE_SIZE
    n_used = DEC_BATCH * n_pages
    n_pool = n_used + max(1, n_used // 4)

    def nrm(k, shape, s):
        return s * jax.random.normal(k, shape, F32)

    layout = _proj_layout()
    total = sum(s for _, s in layout)
    col_scale = jnp.concatenate(
        [jnp.full((s,), BETA if name.endswith("_v") else 1.0, F32) for name, s in layout])

    x_prompt = nrm(ks[0], (BATCH, SEQ, D_MODEL), 1.0)
    x_sample = nrm(ks[1], (DEC_BATCH, DEC_SEQ, D_MODEL), 1.0)
    state_mlstm_c = nrm(ks[2], (DEPTH, DEC_BATCH, H_M, DV_M, DK_M), 0.1)
    state_mlstm_n = nrm(ks[3], (DEPTH, DEC_BATCH, H_M, DK_M), 0.1)
    state_mlstm_m = nrm(ks[4], (DEPTH, DEC_BATCH, H_M), 1.0)
    state_gla = nrm(ks[5], (DEPTH, DEC_BATCH, H_G, DK_G, DV_G), 0.5)
    state_ffn_conv = nrm(ks[6], (DEPTH, DEC_BATCH, CONV_W - 1, 2 * D_FF), 1.0)
    cache_k = nrm(ks[7], (DEPTH, n_pool, PAGE_SIZE, H_F, DH_F), 1.0)
    cache_v = nrm(ks[8], (DEPTH, n_pool, PAGE_SIZE, H_F, DH_F), 1.0)
    cache_logf = jax.nn.log_sigmoid(
        FOX_FORGET_BIAS + 0.5 * jax.random.normal(ks[9], (DEPTH, n_pool, PAGE_SIZE, H_F), F32))
    page_table = jax.random.permutation(ks[10], n_pool)[:n_used].reshape(DEC_BATCH, n_pages).astype(jnp.int32)

    w_in = nrm(ks[11], (DEPTH, D_MODEL, total), D_MODEL ** -0.5) * col_scale
    b_m_ig = nrm(ks[12], (DEPTH, H_M), 0.1)
    b_m_fg = jnp.linspace(3.0, 6.0, H_M, dtype=F32) + nrm(ks[13], (DEPTH, H_M), 0.1)
    w_g_alpha_up = nrm(ks[14], (DEPTH, GLA_RANK, H_G * DK_G), GLA_RANK ** -0.5)
    b_g_alpha = nrm(ks[15], (DEPTH, H_G * DK_G), 0.1)
    b_f = FOX_FORGET_BIAS + nrm(ks[16], (DEPTH, H_F), 0.1)
    g_m_norm = 1.0 + nrm(ks[17], (DEPTH, H_M * DV_M), 0.02)
    g_g_norm = 1.0 + nrm(ks[18], (DEPTH, H_G * DV_G), 0.02)
    w_out = nrm(ks[19], (DEPTH, MIX_WIDTH, D_MODEL), BETA * MIX_WIDTH ** -0.5)
    ln1_g = 1.0 + nrm(ks[20], (DEPTH, D_MODEL), 0.02)
    ln1_b = nrm(ks[21], (DEPTH, D_MODEL), 0.02)
    w_up = nrm(ks[22], (DEPTH, D_MODEL, 2 * D_FF), D_MODEL ** -0.5)
    conv_w = nrm(ks[23], (DEPTH, CONV_W, 2 * D_FF), CONV_W ** -0.5)
    conv_b = nrm(ks[24], (DEPTH, 2 * D_FF), 0.02)
    w_down = nrm(ks[25], (DEPTH, D_FF, D_MODEL), BETA * D_FF ** -0.5)
    ln2_g = 1.0 + nrm(ks[26], (DEPTH, D_MODEL), 0.02)
    ln2_b = nrm(ks[27], (DEPTH, D_MODEL), 0.02)
    return {"x_prompt": x_prompt, "x_sample": x_sample,
            "state_mlstm_c": state_mlstm_c, "state_mlstm_n": state_mlstm_n, "state_mlstm_m": state_mlstm_m,
            "state_gla": state_gla, "state_ffn_conv": state_ffn_conv,
            "cache_k": cache_k, "cache_v": cache_v, "cache_logf": cache_logf, "page_table": page_table,
            "w_in": w_in, "b_m_ig": b_m_ig, "b_m_fg": b_m_fg, "w_g_alpha_up": w_g_alpha_up,
            "b_g_alpha": b_g_alpha, "b_f": b_f, "g_m_norm": g_m_norm, "g_g_norm": g_g_norm, "w_out": w_out,
            "ln1_g": ln1_g, "ln1_b": ln1_b, "w_up": w_up, "conv_w": conv_w, "conv_b": conv_b,
            "w_down": w_down, "ln2_g": ln2_g, "ln2_b": ln2_b}


def reference(x_prompt, x_sample, state_mlstm_c, state_mlstm_n, state_mlstm_m, state_gla, state_ffn_conv,
              cache_k, cache_v, cache_logf, page_table,
              w_in, b_m_ig, b_m_fg, w_g_alpha_up, b_g_alpha, b_f, g_m_norm, g_g_norm, w_out,
              ln1_g, ln1_b, w_up, conv_w, conv_b, w_down, ln2_g, ln2_b):
    xp = x_prompt.astype(F32)
    xs = x_sample.astype(F32)
    bp = xp.shape[0]
    st_p = [[] for _ in range(8)]
    st_s = [[] for _ in range(8)]
    for l in range(DEPTH):
        mix_params = (w_in[l], b_m_ig[l], b_m_fg[l], w_g_alpha_up[l], b_g_alpha[l], b_f[l],
                      g_m_norm[l], g_g_norm[l], w_out[l])
        ffn_params = (w_up[l], conv_w[l], conv_b[l], w_down[l])
        norm_params = (ln1_g[l], ln1_b[l], ln2_g[l], ln2_b[l])
        init_p = (jnp.zeros((bp, H_M, DV_M, DK_M), F32), jnp.zeros((bp, H_M, DK_M), F32),
                  jnp.zeros((bp, H_M), F32), jnp.zeros((bp, H_G, DK_G, DV_G), F32),
                  jnp.zeros((bp, CONV_W - 1, 2 * D_FF), F32))
        xp, new_p = _layer(xp, mix_params, ffn_params, norm_params, init_p, None)
        past = (_gather_pages(cache_k[l], page_table), _gather_pages(cache_v[l], page_table),
                _gather_pages(cache_logf[l], page_table))
        init_s = (state_mlstm_c[l].astype(F32), state_mlstm_n[l].astype(F32), state_mlstm_m[l].astype(F32),
                  state_gla[l].astype(F32), state_ffn_conv[l])
        xs, new_s = _layer(xs, mix_params, ffn_params, norm_params, init_s, past)
        for lst, a in zip(st_p, new_p):
            lst.append(a)
        for lst, a in zip(st_s, new_s):
            lst.append(a)
    (mlstm_c_prompt, mlstm_n_prompt, mlstm_m_prompt, gla_s_prompt, ffn_conv_prompt,
     fox_k_prompt, fox_v_prompt, fox_logf_prompt) = [jnp.stack(a) for a in st_p]
    (mlstm_c_sample, mlstm_n_sample, mlstm_m_sample, gla_s_sample, ffn_conv_sample,
     fox_k_sample, fox_v_sample, fox_logf_sample) = [jnp.stack(a) for a in st_s]
    return (xp, xs,
            mlstm_c_prompt, mlstm_n_prompt, mlstm_m_prompt, gla_s_prompt, ffn_conv_prompt,
            fox_k_prompt, fox_v_prompt, fox_logf_prompt,
            mlstm_c_sample, mlstm_n_sample, mlstm_m_sample, gla_s_sample, ffn_conv_sample,
            fox_k_sample, fox_v_sample, fox_logf_sample)
```

```python
import functools

import numpy as np
import jax
import jax.numpy as jnp
from jax import lax
from jax.experimental import pallas as pl
from jax.experimental.pallas import tpu as pltpu

F32 = jnp.float32
BF16 = jnp.bfloat16

HEAD_DIM = 128
GLA_RANK = 16
GLA_TAU = 16.0
CONV_W = 3
LN_EPS = 1e-5
NORM_EPS = 1e-6

LANES = 128
SUBLANES = 8
VMEM_LIMIT = 56 * 1024 * 1024

CHUNK = 128
GATE_TILE = 256
ATTN_TILE = 512
FFN_ROW_TILE = 1024
NEG_INF = float("-inf")


def _params(*sem):
    return pltpu.CompilerParams(dimension_semantics=sem, vmem_limit_bytes=VMEM_LIMIT)


def _dot(a, b):
    return jnp.dot(a, b, preferred_element_type=F32)


def _dot_nt(a, b):
    return lax.dot_general(a, b, (((1,), (1,)), ((), ())), preferred_element_type=F32)


def _dot_tn(a, b):
    return lax.dot_general(a, b, (((0,), (0,)), ((), ())), preferred_element_type=F32)


def _log_sigmoid(x):
    return jnp.minimum(x, 0.0) - jnp.log1p(jnp.exp(-jnp.abs(x)))


def _sigmoid(x):
    return 1.0 / (1.0 + jnp.exp(-x))


def _split3(x):
    hi = x.astype(BF16)
    r = x - hi.astype(F32)
    mid = r.astype(BF16)
    lo = (r - mid.astype(F32)).astype(BF16)
    return hi, mid, lo


def _mm_kernel(x_ref, w_ref, *o_refs):
    acc = _dot(x_ref[...], w_ref[...])
    for o in o_refs:
        o[...] = acc.astype(o.dtype)


def _matmul(x, w, out_dtypes, tm, tn, name):
    m, k = x.shape
    n = w.shape[1]
    tm = min(tm, m)
    tn = min(tn, n)
    assert m % tm == 0 and n % tn == 0
    return pl.pallas_call(
        _mm_kernel,
        grid=(m // tm, n // tn),
        in_specs=[pl.BlockSpec((tm, k), lambda i, j: (i, 0)),
                  pl.BlockSpec((k, tn), lambda i, j: (0, j))],
        out_specs=[pl.BlockSpec((tm, tn), lambda i, j: (i, j)) for _ in out_dtypes],
        out_shape=[jax.ShapeDtypeStruct((m, n), d) for d in out_dtypes],
        compiler_params=_params("parallel", "parallel"),
        name=name,
    )(x, w)


def _gates_kernel(x_ref, w_ref, ba_ref, bb_ref, tri_ref, oa_ref, ob_ref, oc_ref, carry, *, tiles_per_seq, n_ig):
    i = pl.program_id(0)
    tm = x_ref.shape[0]
    z = _dot(x_ref[...], w_ref[...])
    za = z[:, :LANES] + ba_ref[...]
    ls = _log_sigmoid(z[:, LANES:] + bb_ref[...])
    tri = tri_ref[...]
    hi, mid, lo = _split3(ls)
    cum = _dot(tri, hi) + _dot(tri, mid) + _dot(tri, lo)
    if tiles_per_seq > 1:
        @pl.when(i % tiles_per_seq == 0)
        def _():
            carry[...] = jnp.zeros_like(carry)
        cum = cum + carry[0:1, :]
        carry[...] = jnp.broadcast_to(cum[tm - 1:tm, :], carry.shape)
    lane = lax.broadcasted_iota(jnp.int32, (tm, LANES), 1)
    oa_ref[...] = jnp.where(lane < n_ig, za - cum, za)
    ob_ref[...] = cum
    oc_ref[...] = ls


def _gates(x, w_small, bias_a, bias_b, tri, tiles_per_seq, n_ig):
    n, d = x.shape
    tm = tri.shape[0]
    assert n % tm == 0
    out = jax.ShapeDtypeStruct((n, LANES), F32)
    return pl.pallas_call(
        functools.partial(_gates_kernel, tiles_per_seq=tiles_per_seq, n_ig=n_ig),
        grid=(n // tm,),
        in_specs=[pl.BlockSpec((tm, d), lambda i: (i, 0)),
                  pl.BlockSpec((d, 2 * LANES), lambda i: (0, 0)),
                  pl.BlockSpec((1, LANES), lambda i: (0, 0)),
                  pl.BlockSpec((1, LANES), lambda i: (0, 0)),
                  pl.BlockSpec((tm, tm), lambda i: (0, 0))],
        out_specs=[pl.BlockSpec((tm, LANES), lambda i: (i, 0))] * 3,
        out_shape=[out, out, out],
        scratch_shapes=[pltpu.VMEM((SUBLANES, LANES), F32)],
        compiler_params=_params("arbitrary"),
        name="gates",
    )(x, w_small, bias_a, bias_b, tri)


def _mlstm_kernel(q_ref, k_ref, v_ref, o_ref, ga_ref, gb_ref, arow_ref, c0_ref, n0_ref, m0_ref, g_ref,
                  h_ref, c_ref, n_ref, m_ref, ct_sc, n_sc, a_sc, *, n_heads):
    bh = pl.program_id(0)
    c = pl.program_id(1)
    hd = bh % n_heads
    L = q_ref.shape[0]
    dk = q_ref.shape[1]

    @pl.when(c == 0)
    def _():
        ct_sc[...] = c0_ref[...]
        n_sc[...] = jnp.broadcast_to(n0_ref[...], n_sc.shape)
        a_sc[...] = jnp.broadcast_to(m0_ref[...], a_sc.shape)

    q = q_ref[...]
    k = k_ref[...]
    v = v_ref[...]
    lane = lax.broadcasted_iota(jnp.int32, (L, LANES), 1)
    sel = lane == hd
    a_col = jnp.sum(jnp.where(sel, ga_ref[...], 0.0), axis=1, keepdims=True)
    b_col = jnp.sum(jnp.where(sel, gb_ref[...], 0.0), axis=1, keepdims=True)
    a_row = arow_ref[...]
    a_prev = a_sc[0:1, 0:1]
    ti = lax.broadcasted_iota(jnp.int32, (L, L), 0)
    si = lax.broadcasted_iota(jnp.int32, (L, L), 1)
    mm = jnp.where(si <= ti, a_row, NEG_INF)
    a_t = jnp.maximum(jnp.max(mm, axis=1, keepdims=True), a_prev)
    s = _dot_nt(q, k) * (dk ** -0.5) * jnp.exp(mm - a_t)
    inter = jnp.exp(a_prev - a_t)
    num = _dot(s.astype(BF16), v) + inter * _dot(q, ct_sc[...].astype(BF16))
    den = (jnp.sum(s, axis=1, keepdims=True)
           + inter * jnp.sum(q.astype(F32) * n_sc[0:1, :], axis=1, keepdims=True))
    h = num / jnp.maximum(jnp.abs(den), jnp.exp(-(b_col + a_t)))
    hm = _sigmoid(o_ref[...].astype(F32)) * h
    hm = hm * lax.rsqrt(jnp.mean(hm * hm, axis=1, keepdims=True) + NORM_EPS) * g_ref[...]
    h_ref[...] = hm.astype(h_ref.dtype)

    a_end = jnp.maximum(jnp.max(a_row, axis=1, keepdims=True), a_prev)
    e_col = jnp.exp(a_col - a_end) * (dk ** -0.5)
    decay = jnp.exp(a_prev - a_end)
    ke = k.astype(F32) * e_col
    ct_new = decay * ct_sc[...] + _dot_tn(ke.astype(BF16), v)
    n_new = decay * n_sc[0:1, :] + jnp.sum(ke, axis=0, keepdims=True)
    ct_sc[...] = ct_new
    n_sc[...] = jnp.broadcast_to(n_new, n_sc.shape)
    a_sc[...] = jnp.broadcast_to(a_end, a_sc.shape)

    @pl.when(c == pl.num_programs(1) - 1)
    def _():
        c_ref[...] = ct_new
        n_ref[...] = n_new
        m_ref[...] = jnp.broadcast_to(b_col[L - 1:L, :] + a_end, m_ref.shape)


def _mlstm(zm, ga, gb, a_rows, c0t, n0, m0, g_norm, batch, n_heads, seq):
    L = CHUNK
    nc = seq // L
    n = batch * seq
    bhn = batch * n_heads
    dh = HEAD_DIM

    def col(off):
        return pl.BlockSpec((L, dh), lambda bh, c: ((bh // n_heads) * nc + c, off * n_heads + bh % n_heads))

    def rows():
        return pl.BlockSpec((L, LANES), lambda bh, c: ((bh // n_heads) * nc + c, 0))

    def per_bh(r, w):
        return pl.BlockSpec((None, r, w), lambda bh, c: (bh, 0, 0))

    return pl.pallas_call(
        functools.partial(_mlstm_kernel, n_heads=n_heads),
        grid=(bhn, nc),
        in_specs=[col(0), col(1), col(2), col(3), rows(), rows(),
                  pl.BlockSpec((None, 1, L), lambda bh, c: (bh, 0, c)),
                  per_bh(dh, dh), per_bh(1, dh), per_bh(1, LANES),
                  pl.BlockSpec((1, dh), lambda bh, c: (0, bh % n_heads))],
        out_specs=[pl.BlockSpec((L, dh), lambda bh, c: ((bh // n_heads) * nc + c, bh % n_heads)),
                   per_bh(dh, dh), per_bh(1, dh), per_bh(1, LANES)],
        out_shape=[jax.ShapeDtypeStruct((n, n_heads * dh), BF16),
                   jax.ShapeDtypeStruct((bhn, dh, dh), F32),
                   jax.ShapeDtypeStruct((bhn, 1, dh), F32),
                   jax.ShapeDtypeStruct((bhn, 1, LANES), F32)],
        scratch_shapes=[pltpu.VMEM((dh, dh), F32), pltpu.VMEM((SUBLANES, dh), F32),
                        pltpu.VMEM((SUBLANES, LANES), F32)],
        compiler_params=_params("parallel", "arbitrary"),
        name="mlstm",
    )(zm, zm, zm, zm, ga, gb, a_rows, c0t, n0, m0, g_norm)


def _gla_levels(L):
    levels = []
    w = L // 2
    while w >= 1:
        levels.append(w)
        w //= 2
    return levels


def _gla_consts(L):
    t = np.arange(L)
    row, colj = t[:, None], t[None, :]
    mats = [(colj <= row), (colj > row)]
    masks = []
    for w in _gla_levels(L):
        mid = (t // (2 * w)) * 2 * w + w
        right = t >= mid
        mr = right[:, None] & (colj >= mid[:, None]) & (colj <= row)
        ml = (~right)[:, None] & (colj > row) & (colj < mid[:, None])
        mats.append(mr | ml)
        same = (t[:, None] // (2 * w)) == (t[None, :] // (2 * w))
        masks.append(same & right[:, None] & (~right)[None, :])
    masks.append(row == colj)
    m_all = np.concatenate([m.astype(np.float32) for m in mats], axis=0)
    return m_all, np.stack([m.astype(np.float32) for m in masks])


def _gla_kernel(q_ref, k_ref, v_ref, r_ref, ga_ref, wa_ref, ba_ref, mall_ref, mask_ref, s0_ref, g_ref,
                h_ref, s_ref, st_sc, *, n_valid):
    c = pl.program_id(2)
    L = q_ref.shape[0]
    dk = LANES // 2
    dv = HEAD_DIM
    n_lev = mask_ref.shape[0] - 1

    @pl.when(c == 0)
    def _():
        st_sc[...] = s0_ref[...]

    q2 = q_ref[...].astype(F32) * (dk ** -0.5)
    k2 = k_ref[...].astype(F32)
    la = _log_sigmoid(_dot(ga_ref[...].astype(BF16), wa_ref[...]) + ba_ref[...]) * (1.0 / GLA_TAU)
    if n_valid < L:
        valid = lax.broadcasted_iota(jnp.int32, (L, LANES), 0) < n_valid
        la = jnp.where(valid, la, 0.0)
        k2 = jnp.where(valid, k2, 0.0)
    la_hi = la.astype(BF16)
    la_mid = (la - la_hi.astype(F32)).astype(BF16)
    mall = mall_ref[...]
    e = jnp.exp(_dot(mall, la_hi) + _dot(mall, la_mid))
    lane = lax.broadcasted_iota(jnp.int32, (1, LANES), 1)
    lm = [(lane < dk).astype(F32), (lane >= dk).astype(F32)]

    q_in = q2 * e[0:L]
    k_end = (k2 * e[L:2 * L]).astype(BF16)
    decay = e[L - 1:L]
    kq = [(q2 * e[(2 + i) * L:(3 + i) * L], (k2 * e[(2 + i) * L:(3 + i) * L]).astype(BF16))
          for i in range(n_lev)]
    k2b = k2.astype(BF16)
    outs = []
    for hh in range(2):
        v = v_ref[:, hh * dv:(hh + 1) * dv]
        st = st_sc[hh]
        o = _dot_nt((q_in * lm[hh]).astype(BF16), st.astype(BF16))
        a = mask_ref[n_lev] * _dot_nt((q2 * lm[hh]).astype(BF16), k2b)
        for i in range(n_lev):
            qh, kh = kq[i]
            a = a + mask_ref[i] * _dot_nt((qh * lm[hh]).astype(BF16), kh)
        o = o + _dot(a.astype(BF16), v)
        st_new = decay * st + _dot_tn(v, k_end)
        st_sc[hh] = st_new
        o = o * lax.rsqrt(jnp.mean(o * o, axis=1, keepdims=True) + NORM_EPS) * g_ref[:, hh * dv:(hh + 1) * dv]
        r = r_ref[:, hh * dv:(hh + 1) * dv].astype(F32)
        outs.append((r * _sigmoid(r) * o).astype(h_ref.dtype))
    h_ref[...] = jnp.concatenate(outs, axis=1)

    @pl.when(c == pl.num_programs(2) - 1)
    def _():
        s_ref[...] = st_sc[...]


def _gla(zg, ga, wa, ba, s0t, g_norm, batch, n_heads, seq, n_valid):
    L = CHUNK
    nc = seq // L
    n = batch * seq
    npair = n_heads // 2
    dv = HEAD_DIM
    m_all, masks = _gla_consts(L)
    m_all = jnp.asarray(m_all, BF16)
    masks = jnp.asarray(masks, F32)
    qk_blocks = npair
    vr_blocks = npair

    def rowblk(b, c):
        return b * nc + c

    return pl.pallas_call(
        functools.partial(_gla_kernel, n_valid=n_valid),
        grid=(batch, npair, nc),
        in_specs=[pl.BlockSpec((L, LANES), lambda b, p, c: (rowblk(b, c), p)),
                  pl.BlockSpec((L, LANES), lambda b, p, c: (rowblk(b, c), qk_blocks + p)),
                  pl.BlockSpec((L, 2 * dv), lambda b, p, c: (rowblk(b, c), qk_blocks + p)),
                  pl.BlockSpec((L, 2 * dv), lambda b, p, c: (rowblk(b, c), qk_blocks + vr_blocks + p)),
                  pl.BlockSpec((L, LANES), lambda b, p, c: (rowblk(b, c), 0)),
                  pl.BlockSpec((None, LANES, LANES), lambda b, p, c: (p, 0, 0)),
                  pl.BlockSpec((None, 1, LANES), lambda b, p, c: (p, 0, 0)),
                  pl.BlockSpec(m_all.shape, lambda b, p, c: (0, 0)),
                  pl.BlockSpec(masks.shape, lambda b, p, c: (0, 0, 0)),
                  pl.BlockSpec((None, None, 2, dv, LANES), lambda b, p, c: (b, p, 0, 0, 0)),
                  pl.BlockSpec((1, 2 * dv), lambda b, p, c: (0, p))],
        out_specs=[pl.BlockSpec((L, 2 * dv), lambda b, p, c: (rowblk(b, c), p)),
                   pl.BlockSpec((None, None, 2, dv, LANES), lambda b, p, c: (b, p, 0, 0, 0))],
        out_shape=[jax.ShapeDtypeStruct((n, n_heads * dv), BF16),
                   jax.ShapeDtypeStruct((batch, npair, 2, dv, LANES), F32)],
        scratch_shapes=[pltpu.VMEM((2, dv, LANES), F32)],
        compiler_params=_params("parallel", "parallel", "arbitrary"),
        name="gla",
    )(zg, zg, zg, zg, ga, wa, ba, m_all, masks, s0t, g_norm)


def _fox_prompt_kernel(qi_ref, ki_ref, q_ref, k_ref, v_ref, gb_ref, crow_ref, o_ref, m_sc, l_sc, acc_sc,
                       *, n_heads, lane0):
    bh = pl.program_id(0)
    p = pl.program_id(1)
    qi = qi_ref[p]
    ki = ki_ref[p]
    hd = bh % n_heads
    tq = q_ref.shape[0]
    tk = k_ref.shape[0]
    d = q_ref.shape[1]

    @pl.when(ki == 0)
    def _():
        m_sc[...] = jnp.full_like(m_sc, NEG_INF)
        l_sc[...] = jnp.zeros_like(l_sc)
        acc_sc[...] = jnp.zeros_like(acc_sc)

    lane = lax.broadcasted_iota(jnp.int32, (tq, LANES), 1)
    c_col = jnp.sum(jnp.where(lane == lane0 + hd, gb_ref[...], 0.0), axis=1, keepdims=True)
    s = _dot_nt(q_ref[...], k_ref[...]) * (d ** -0.5) + (c_col - crow_ref[...])

    def update(s):
        m_old = m_sc[...]
        m_new = jnp.maximum(m_old, jnp.max(s, axis=1, keepdims=True))
        alpha = jnp.exp(m_old - m_new)
        pr = jnp.exp(s - m_new)
        l_sc[...] = alpha * l_sc[...] + jnp.sum(pr, axis=1, keepdims=True)
        acc_sc[...] = alpha * acc_sc[...] + _dot(pr.astype(BF16), v_ref[...])
        m_sc[...] = m_new

    @pl.when(ki < qi)
    def _():
        update(s)

    @pl.when(ki == qi)
    def _():
        ti = lax.broadcasted_iota(jnp.int32, (tq, tk), 0)
        si = lax.broadcasted_iota(jnp.int32, (tq, tk), 1)
        update(jnp.where(si <= ti, s, NEG_INF))
        o_ref[...] = (acc_sc[...] / l_sc[...]).astype(o_ref.dtype)


def _fox_prompt(zq, zk, zv, gb, c_rows, batch, n_heads, seq, tile, lane0):
    n = batch * seq
    nq = seq // tile
    dh = HEAD_DIM
    pairs = [(a, b) for a in range(nq) for b in range(a + 1)]
    qi = jnp.asarray([a for a, _ in pairs], jnp.int32)
    ki = jnp.asarray([b for _, b in pairs], jnp.int32)

    def qrow(bh, p, qi, ki):
        return (bh // n_heads) * nq + qi[p]

    def krow(bh, p, qi, ki):
        return (bh // n_heads) * nq + ki[p]

    grid_spec = pltpu.PrefetchScalarGridSpec(
        num_scalar_prefetch=2,
        grid=(batch * n_heads, len(pairs)),
        in_specs=[pl.BlockSpec((tile, dh), lambda bh, p, qi, ki: (qrow(bh, p, qi, ki), bh % n_heads)),
                  pl.BlockSpec((tile, dh), lambda bh, p, qi, ki: (krow(bh, p, qi, ki), bh % n_heads)),
                  pl.BlockSpec((tile, dh), lambda bh, p, qi, ki: (krow(bh, p, qi, ki), bh % n_heads)),
                  pl.BlockSpec((tile, LANES), lambda bh, p, qi, ki: (qrow(bh, p, qi, ki), 0)),
                  pl.BlockSpec((None, 1, tile), lambda bh, p, qi, ki: (bh, 0, ki[p]))],
        out_specs=pl.BlockSpec((tile, dh), lambda bh, p, qi, ki: (qrow(bh, p, qi, ki), bh % n_heads)),
        scratch_shapes=[pltpu.VMEM((tile, 1), F32), pltpu.VMEM((tile, 1), F32), pltpu.VMEM((tile, dh), F32)],
    )
    return pl.pallas_call(
        functools.partial(_fox_prompt_kernel, n_heads=n_heads, lane0=lane0),
        grid_spec=grid_spec,
        out_shape=jax.ShapeDtypeStruct((n, n_heads * dh), BF16),
        compiler_params=_params("parallel", "arbitrary"),
        name="fox_prompt",
    )(qi, ki, zq, zk, zv, gb, c_rows)


def _fox_sample_kernel(pt_ref, q_ref, cq_ref, kn_ref, vn_ref, bn_ref, k_ref, v_ref, lf_ref, o_ref,
                       m_sc, l_sc, acc_sc, carry_sc, *, n_heads, n_new):
    j = pl.program_id(1)
    nr = q_ref.shape[0]
    rows_pp = k_ref.shape[0]
    d = q_ref.shape[1]
    n_tiles = rows_pp // LANES
    scale = d ** -0.5

    @pl.when(j == 0)
    def _():
        m_sc[...] = jnp.full_like(m_sc, NEG_INF)
        l_sc[...] = jnp.zeros_like(l_sc)
        acc_sc[...] = jnp.zeros_like(acc_sc)
        carry_sc[...] = jnp.zeros_like(carry_sc)

    x = lf_ref[...]
    lane8 = lax.broadcasted_iota(jnp.int32, (SUBLANES, LANES), 1)
    row8 = lax.broadcasted_iota(jnp.int32, (SUBLANES, LANES), 0)
    y = x
    z = x
    sh = n_heads
    while sh < LANES:
        y = y + jnp.where(lane8 + sh < LANES, pltpu.roll(y, LANES - sh, 1), 0.0)
        z = z + pltpu.roll(z, sh, 1)
        sh *= 2
    w = z
    sh = 1
    while sh < SUBLANES:
        w = w + jnp.where(row8 + sh < SUBLANES, pltpu.roll(w, SUBLANES - sh, 0), 0.0)
        sh *= 2
    carry = carry_sc[...]
    suffix = y - x + (w - z) + carry
    carry_sc[...] = carry + jnp.broadcast_to(w[0:1, :], carry.shape)

    q = q_ref[...]
    cq = cq_ref[...]
    rowi = lax.broadcasted_iota(jnp.int32, (nr, LANES), 0)
    lanei = lax.broadcasted_iota(jnp.int32, (nr, LANES), 1)
    head_ok = (rowi // n_new) == (lanei % n_heads)

    def update(logit_tiles, vb):
        m_old = m_sc[...]
        mx = logit_tiles[0]
        for t in logit_tiles[1:]:
            mx = jnp.maximum(mx, t)
        m_new = jnp.maximum(m_old, jnp.max(mx, axis=1, keepdims=True))
        alpha = jnp.exp(m_old - m_new)
        ps = [jnp.exp(t - m_new) for t in logit_tiles]
        tot = ps[0]
        for t in ps[1:]:
            tot = tot + t
        l_sc[...] = alpha * l_sc[...] + jnp.sum(tot, axis=1, keepdims=True)
        pcat = ps[0] if len(ps) == 1 else jnp.concatenate(ps, axis=1)
        acc_sc[...] = alpha * acc_sc[...] + _dot(pcat.astype(BF16), vb)
        m_sc[...] = m_new

    s = _dot_nt(q, k_ref[...].astype(BF16)) * scale
    tiles = [jnp.where(head_ok, s[:, r * LANES:(r + 1) * LANES] + cq + suffix[r:r + 1, :], NEG_INF)
             for r in range(n_tiles)]
    update(tiles, v_ref[...].astype(BF16))

    @pl.when(j == pl.num_programs(1) - 1)
    def _():
        s_self = _dot_nt(q, kn_ref[...]) * scale + cq + bn_ref[...]
        ok = head_ok & (lanei < n_new * n_heads) & ((lanei // n_heads) <= (rowi % n_new))
        update([jnp.where(ok, s_self, NEG_INF)], vn_ref[...])
        o_ref[...] = acc_sc[...] / l_sc[...]


def _fox_sample(page_table, q, cq, kn, vn, bn, cache_k, cache_v, cache_lf, n_heads, n_new):
    nb, n_pages = page_table.shape
    nr = q.shape[1]
    rows_pp = cache_k.shape[1]
    d = q.shape[2]

    def page(b, j, pt):
        return pt[b, n_pages - 1 - j]

    grid_spec = pltpu.PrefetchScalarGridSpec(
        num_scalar_prefetch=1,
        grid=(nb, n_pages),
        in_specs=[pl.BlockSpec((None, nr, d), lambda b, j, pt: (b, 0, 0)),
                  pl.BlockSpec((None, nr, LANES), lambda b, j, pt: (b, 0, 0)),
                  pl.BlockSpec((None, LANES, d), lambda b, j, pt: (b, 0, 0)),
                  pl.BlockSpec((None, LANES, d), lambda b, j, pt: (b, 0, 0)),
                  pl.BlockSpec((None, 1, LANES), lambda b, j, pt: (b, 0, 0)),
                  pl.BlockSpec((None, rows_pp, d), lambda b, j, pt: (page(b, j, pt), 0, 0)),
                  pl.BlockSpec((None, rows_pp, d), lambda b, j, pt: (page(b, j, pt), 0, 0)),
                  pl.BlockSpec((None, SUBLANES, LANES), lambda b, j, pt: (page(b, j, pt), 0, 0))],
        out_specs=pl.BlockSpec((None, nr, d), lambda b, j, pt: (b, 0, 0)),
        scratch_shapes=[pltpu.VMEM((nr, 1), F32), pltpu.VMEM((nr, 1), F32), pltpu.VMEM((nr, d), F32),
                        pltpu.VMEM((SUBLANES, LANES), F32)],
    )
    return pl.pallas_call(
        functools.partial(_fox_sample_kernel, n_heads=n_heads, n_new=n_new),
        grid_spec=grid_spec,
        out_shape=jax.ShapeDtypeStruct((nb, nr, d), F32),
        compiler_params=_params("parallel", "arbitrary"),
        name="fox_sample",
    )(page_table, q, cq, kn, vn, bn, cache_k, cache_v, cache_lf)


def _layer_norm(r, g, b):
    mu = jnp.mean(r, axis=1, keepdims=True)
    xc = r - mu
    var = jnp.mean(xc * xc, axis=1, keepdims=True)
    return xc * lax.rsqrt(var + LN_EPS) * g + b


def _outproj_kernel(hm_ref, hg_ref, hf_ref, wm_ref, wg_ref, wf_ref, x_ref, g_ref, b_ref, h_ref, hb_ref, *, alpha):
    mix = _dot(hm_ref[...], wm_ref[...]) + _dot(hg_ref[...], wg_ref[...]) + _dot(hf_ref[...], wf_ref[...])
    h = _layer_norm(alpha * x_ref[...] + mix, g_ref[...], b_ref[...])
    h_ref[...] = h
    hb_ref[...] = h.astype(BF16)


def _outproj_ln(hm, hg, hf, w_out, x, g, b, alpha, tm):
    n, d = x.shape
    tm = min(tm, n)
    wm, wg, wf = hm.shape[1], hg.shape[1], hf.shape[1]
    assert wm == wg and wf == wm + wg and n % tm == 0
    const = pl.Buffered(1)
    return pl.pallas_call(
        functools.partial(_outproj_kernel, alpha=alpha),
        grid=(n // tm,),
        in_specs=[pl.BlockSpec((tm, wm), lambda i: (i, 0)),
                  pl.BlockSpec((tm, wg), lambda i: (i, 0)),
                  pl.BlockSpec((tm, wf), lambda i: (i, 0)),
                  pl.BlockSpec((wm, d), lambda i: (0, 0), pipeline_mode=const),
                  pl.BlockSpec((wg, d), lambda i: (1, 0), pipeline_mode=const),
                  pl.BlockSpec((wf, d), lambda i: (1, 0), pipeline_mode=const),
                  pl.BlockSpec((tm, d), lambda i: (i, 0)),
                  pl.BlockSpec((1, d), lambda i: (0, 0)),
                  pl.BlockSpec((1, d), lambda i: (0, 0))],
        out_specs=[pl.BlockSpec((tm, d), lambda i: (i, 0))] * 2,
        out_shape=[jax.ShapeDtypeStruct((n, d), F32), jax.ShapeDtypeStruct((n, d), BF16)],
        compiler_params=_params("parallel"),
        name="outproj_ln",
    )(hm, hg, hf, w_out, w_out, w_out, x, g, b)


def _ffn_up_kernel(x_ref, wg_ref, wv_ref, cwg_ref, cwv_ref, cbg_ref, cbv_ref, hg0_ref, hv0_ref,
                   o_ref, tg_ref, tv_ref, halo_g, halo_v, *, tiles_per_seq, shift):
    i = pl.program_id(0)
    j = pl.program_id(1)
    tm = x_ref.shape[0]
    hr = hg0_ref.shape[0]
    x = x_ref[...]
    first = i % tiles_per_seq == 0

    def branch(w_ref, cw_ref, cb_ref, h0_ref, halo, t_ref):
        @pl.when(first)
        def _():
            halo[j] = h0_ref[...]

        u = _dot(x, w_ref[...])
        ext = jnp.concatenate([halo[j], u], axis=0)
        tail = ext[tm:tm + hr]
        halo[j] = tail
        t_ref[...] = tail
        cw = cw_ref[...]
        return (cb_ref[...] + cw[0:1] * ext[hr - 2 * shift:hr - 2 * shift + tm]
                + cw[1:2] * ext[hr - shift:hr - shift + tm] + cw[2:3] * u)

    yg = branch(wg_ref, cwg_ref, cbg_ref, hg0_ref, halo_g, tg_ref)
    yv = branch(wv_ref, cwv_ref, cbv_ref, hv0_ref, halo_v, tv_ref)
    o_ref[...] = (yg * _sigmoid(yg) * yv).astype(o_ref.dtype)


def _ffn_up(x, w_up, conv_w, conv_b, halo0, fp, tm, tn, tiles_per_seq, shift):
    n, d = x.shape
    tm = min(tm, n)
    nj = fp // tn
    ni = n // tm
    hr = halo0.shape[0] // (ni // tiles_per_seq)
    assert n % tm == 0 and fp % tn == 0 and hr >= 2 * shift
    seq = lambda i: i // tiles_per_seq
    return pl.pallas_call(
        functools.partial(_ffn_up_kernel, tiles_per_seq=tiles_per_seq, shift=shift),
        grid=(ni, nj),
        in_specs=[pl.BlockSpec((tm, d), lambda i, j: (i, 0)),
                  pl.BlockSpec((d, tn), lambda i, j: (0, j)),
                  pl.BlockSpec((d, tn), lambda i, j: (0, nj + j)),
                  pl.BlockSpec((CONV_W, tn), lambda i, j: (0, j)),
                  pl.BlockSpec((CONV_W, tn), lambda i, j: (0, nj + j)),
                  pl.BlockSpec((1, tn), lambda i, j: (0, j)),
                  pl.BlockSpec((1, tn), lambda i, j: (0, nj + j)),
                  pl.BlockSpec((hr, tn), lambda i, j: (seq(i), j)),
                  pl.BlockSpec((hr, tn), lambda i, j: (seq(i), nj + j))],
        out_specs=[pl.BlockSpec((tm, tn), lambda i, j: (i, j)),
                   pl.BlockSpec((hr, tn), lambda i, j: (i, j)),
                   pl.BlockSpec((hr, tn), lambda i, j: (i, j))],
        out_shape=[jax.ShapeDtypeStruct((n, fp), BF16),
                   jax.ShapeDtypeStruct((ni * hr, fp), F32),
                   jax.ShapeDtypeStruct((ni * hr, fp), F32)],
        scratch_shapes=[pltpu.VMEM((nj, hr, tn), F32), pltpu.VMEM((nj, hr, tn), F32)],
        compiler_params=_params("arbitrary", "arbitrary"),
        name="ffn_up",
    )(x, w_up, w_up, conv_w, conv_w, conv_b, conv_b, halo0, halo0)


def _ffn_down_kernel(a_ref, w_ref, h_ref, g_ref, b_ref, x_ref, xb_ref, acc, *, alpha):
    k = pl.program_id(1)

    @pl.when(k == 0)
    def _():
        acc[...] = jnp.zeros_like(acc)

    acc[...] += _dot(a_ref[...], w_ref[...])

    @pl.when(k == pl.num_programs(1) - 1)
    def _():
        x = _layer_norm(alpha * h_ref[...] + acc[...], g_ref[...], b_ref[...])
        x_ref[...] = x
        xb_ref[...] = x.astype(BF16)


def _ffn_down_ln(a, w_down, h, g, b, alpha, tm, tk):
    n, fp = a.shape
    d = w_down.shape[1]
    tm = min(tm, n)
    assert n % tm == 0 and fp % tk == 0
    return pl.pallas_call(
        functools.partial(_ffn_down_kernel, alpha=alpha),
        grid=(n // tm, fp // tk),
        in_specs=[pl.BlockSpec((tm, tk), lambda i, k: (i, k)),
                  pl.BlockSpec((tk, d), lambda i, k: (k, 0)),
                  pl.BlockSpec((tm, d), lambda i, k: (i, 0)),
                  pl.BlockSpec((1, d), lambda i, k: (0, 0)),
                  pl.BlockSpec((1, d), lambda i, k: (0, 0))],
        out_specs=[pl.BlockSpec((tm, d), lambda i, k: (i, 0))] * 2,
        out_shape=[jax.ShapeDtypeStruct((n, d), F32), jax.ShapeDtypeStruct((n, d), BF16)],
        scratch_shapes=[pltpu.VMEM((tm, d), F32)],
        compiler_params=_params("parallel", "arbitrary"),
        name="ffn_down_ln",
    )(a, w_down, h, g, b)


def _prep_layer(l, dims, w_in, b_m_ig, b_m_fg, w_g_alpha_up, b_g_alpha, b_f, g_m_norm, g_g_norm, w_out,
                ln1_g, ln1_b, w_up, conv_w, conv_b, w_down, ln2_g, ln2_b):
    d, hm, hg, hf, f, fp = dims
    dh = HEAD_DIM
    dkg = dh // 2
    sizes = [hm * dh] * 4 + [hm, hm] + [hg * dkg] * 2 + [hg * dh] * 2 + [GLA_RANK] + [hf * dh] * 3 + [hf]
    offs = np.concatenate([[0], np.cumsum(sizes)]).tolist()
    (o_mq, o_mk, o_mv, o_mo, o_mi, o_mf, o_gq, o_gk, o_gv, o_gr, o_ga, o_fq, o_fk, o_fv, o_ff, _) = offs
    w = w_in[l]
    col = lambda a, b: w[:, a:b]
    zeros = lambda c: jnp.zeros((d, c), F32)
    blk_a = jnp.concatenate([col(o_mi, o_mi + hm), zeros(8 - hm), col(o_ga, o_ga + GLA_RANK),
                             zeros(LANES - 8 - GLA_RANK)], axis=1)
    blk_b = jnp.concatenate([col(o_mf, o_mf + hm), zeros(8 - hm), col(o_ff, o_ff + hf),
                             zeros(LANES - 8 - hf)], axis=1)
    pad1 = lambda v, lo, total: jnp.pad(v, (lo, total - lo - v.shape[0]))[None, :]
    wa = jnp.zeros((LANES, hg * dkg), F32).at[8:8 + GLA_RANK].set(w_g_alpha_up[l])
    npair = hg // 2
    pad_f = lambda a: jnp.pad(a, [(0, 0)] * (a.ndim - 1) + [(0, fp - f)])
    return dict(
        w_m=col(o_mq, o_mi).astype(BF16),
        w_g=col(o_gq, o_ga).astype(BF16),
        w_fq=col(o_fq, o_fk).astype(BF16),
        w_fk=col(o_fk, o_fv).astype(BF16),
        w_fv=col(o_fv, o_ff).astype(BF16),
        w_small=jnp.concatenate([blk_a, blk_b], axis=1).astype(BF16),
        bias_a=pad1(b_m_ig[l], 0, LANES),
        bias_b=pad1(b_m_fg[l], 0, LANES) + pad1(b_f[l], 8, LANES),
        wa=wa.reshape(LANES, npair, LANES).transpose(1, 0, 2).astype(BF16),
        ba=b_g_alpha[l].reshape(npair, 1, LANES),
        g_m=g_m_norm[l][None, :], g_g=g_g_norm[l][None, :],
        w_out=w_out[l].astype(BF16),
        ln1_g=ln1_g[l][None, :], ln1_b=ln1_b[l][None, :], ln2_g=ln2_g[l][None, :], ln2_b=ln2_b[l][None, :],
        w_up=jnp.concatenate([pad_f(w_up[l][:, :f]), pad_f(w_up[l][:, f:])], axis=1).astype(BF16),
        conv_w=jnp.concatenate([pad_f(conv_w[l][:, :f]), pad_f(conv_w[l][:, f:])], axis=1),
        conv_b=jnp.concatenate([pad_f(conv_b[l][:f]), pad_f(conv_b[l][f:])])[None, :],
        w_down=jnp.pad(w_down[l], ((0, fp - f), (0, 0))).astype(BF16),
    )


def _pad_conv_state(s, f, fp):
    pad = [(0, 0)] * (s.ndim - 1) + [(0, fp - f)]
    return jnp.concatenate([jnp.pad(s[..., :f], pad), jnp.pad(s[..., f:], pad)], axis=-1)


def _mixer_rows(x_bf, p, tm):
    zm, = _matmul(x_bf, p["w_m"], [BF16], tm, 512, "proj_m")
    zg, = _matmul(x_bf, p["w_g"], [BF16], tm, 512, "proj_g")
    zq, = _matmul(x_bf, p["w_fq"], [BF16], tm, 512, "proj_fq")
    zk, zkb = _matmul(x_bf, p["w_fk"], [F32, BF16], tm, 512, "proj_fk")
    zv, zvb = _matmul(x_bf, p["w_fv"], [F32, BF16], tm, 512, "proj_fv")
    return zm, zg, zq, zk, zkb, zv, zvb


def kernel(x_prompt, x_sample, state_mlstm_c, state_mlstm_n, state_mlstm_m, state_gla, state_ffn_conv,
           cache_k, cache_v, cache_logf, page_table,
           w_in, b_m_ig, b_m_fg, w_g_alpha_up, b_g_alpha, b_f, g_m_norm, g_g_norm, w_out,
           ln1_g, ln1_b, w_up, conv_w, conv_b, w_down, ln2_g, ln2_b):
    bp, seq, d = x_prompt.shape
    db, ns, _ = x_sample.shape
    depth = w_in.shape[0]
    hm = b_m_ig.shape[1]
    hf = b_f.shape[1]
    dh = HEAD_DIM
    hg = g_g_norm.shape[1] // dh
    dkg = dh // 2
    f = w_down.shape[1]
    fp = -(-f // 512) * 512
    alpha = (2.0 * depth) ** 0.25
    n_pool, page = cache_k.shape[1], cache_k.shape[2]
    assert hm <= 8 and hf <= 8 and hg % 2 == 0 and seq % 512 == 0 and ns <= CHUNK
    assert page * hf == SUBLANES * LANES and ns * hf <= LANES
    dims = (d, hm, hg, hf, f, fp)
    L = CHUNK
    n_p = bp * seq

    tri_p = jnp.asarray(np.tril(np.ones((GATE_TILE, GATE_TILE), np.float32)), BF16)
    gs_rows = LANES
    r = np.arange(gs_rows)
    tri_s_np = ((r[:, None] % db == r[None, :] % db) & (r[None, :] // db <= r[:, None] // db)
                & (r[:, None] < ns * db) & (r[None, :] < ns * db))
    tri_s = jnp.asarray(tri_s_np.astype(np.float32), BF16)

    xp = x_prompt.astype(F32).reshape(n_p, d)
    xp_bf = xp.astype(BF16)
    xs = x_sample.astype(F32).transpose(1, 0, 2).reshape(ns * db, d)
    xs_bf = xs.astype(BF16)

    ck = cache_k.astype(F32).reshape(depth, n_pool, page * hf, dh)
    cv = cache_v.astype(F32).reshape(depth, n_pool, page * hf, dh)
    clf = cache_logf.astype(F32).reshape(depth, n_pool, SUBLANES, LANES)

    def to_padded(z, pad_value=0.0, mode="constant"):
        c = z.shape[-1]
        a = z.reshape(ns, db, c).transpose(1, 0, 2)
        if mode == "edge":
            a = jnp.pad(a, ((0, 0), (0, L - ns), (0, 0)), mode="edge")
        else:
            a = jnp.pad(a, ((0, 0), (0, L - ns), (0, 0)), constant_values=pad_value)
        return a.reshape(db * L, c)

    def from_padded(y):
        c = y.shape[-1]
        return y.reshape(db, L, c)[:, :ns].transpose(1, 0, 2).reshape(ns * db, c)

    def head_rows(g, lane0, nh, batch, t):
        return g[:, lane0:lane0 + nh].reshape(batch, t, nh).transpose(0, 2, 1).reshape(batch * nh, 1, t)

    outs_p = [[] for _ in range(8)]
    outs_s = [[] for _ in range(8)]
    for l in range(depth):
        p = _prep_layer(l, dims, w_in, b_m_ig, b_m_fg, w_g_alpha_up, b_g_alpha, b_f, g_m_norm, g_g_norm, w_out,
                        ln1_g, ln1_b, w_up, conv_w, conv_b, w_down, ln2_g, ln2_b)

        zm, zg, zq, zk, zkb, zv, zvb = _mixer_rows(xp_bf, p, 1024)
        ga, gb, gc = _gates(xp_bf, p["w_small"], p["bias_a"], p["bias_b"], tri_p, seq // GATE_TILE, hm)
        zeros = lambda *s: jnp.zeros(s, F32)
        h_m, ct, n_m, m_m = _mlstm(zm, ga, gb, head_rows(ga, 0, hm, bp, seq),
                                   zeros(bp * hm, dh, dh), zeros(bp * hm, 1, dh), zeros(bp * hm, 1, LANES),
                                   p["g_m"], bp, hm, seq)
        h_g, st = _gla(zg, ga, p["wa"], p["ba"], zeros(bp, hg // 2, 2, dh, LANES), p["g_g"], bp, hg, seq, L)
        h_f = _fox_prompt(zq, zkb, zvb, gb, head_rows(gb, 8, hf, bp, seq), bp, hf, seq, ATTN_TILE, 8)
        h, h_bf = _outproj_ln(h_m, h_g, h_f, p["w_out"], xp, p["ln1_g"], p["ln1_b"], alpha, 512)
        tm_f = min(FFN_ROW_TILE, seq)
        act, tg, tv = _ffn_up(h_bf, p["w_up"], p["conv_w"], p["conv_b"], zeros(bp * SUBLANES, 2 * fp),
                              fp, tm_f, 512, seq // tm_f, 1)
        xp, xp_bf = _ffn_down_ln(act, p["w_down"], h, p["ln2_g"], p["ln2_b"], alpha, 512, fp // 4)

        def conv_tail_p(t):
            t = t.reshape(bp, seq // tm_f, SUBLANES, fp)[:, -1, SUBLANES - (CONV_W - 1):, :f]
            return t
        outs_p[0].append(ct.reshape(bp, hm, dh, dh).swapaxes(-1, -2))
        outs_p[1].append(n_m.reshape(bp, hm, dh))
        outs_p[2].append(m_m[:, 0, 0].reshape(bp, hm))
        st_h = st.reshape(bp, hg // 2, 2, dh, 2, dkg)
        st_h = jnp.stack([st_h[:, :, 0, :, 0, :], st_h[:, :, 1, :, 1, :]], axis=2)
        outs_p[3].append(st_h.reshape(bp, hg, dh, dkg).swapaxes(-1, -2))
        outs_p[4].append(jnp.concatenate([conv_tail_p(tg), conv_tail_p(tv)], axis=-1))
        outs_p[5].append(zk.reshape(bp, seq, hf, dh))
        outs_p[6].append(zv.reshape(bp, seq, hf, dh))
        outs_p[7].append(gc[:, 8:8 + hf].reshape(bp, seq, hf))

        n_s = ns * db
        zm, zg, zq, zk, zkb, zv, zvb = _mixer_rows(xs_bf, p, n_s)
        xs_pad = jnp.pad(xs_bf, ((0, gs_rows - n_s), (0, 0)))
        ga, gb, gc = _gates(xs_pad, p["w_small"], p["bias_a"], p["bias_b"], tri_s, 1, hm)
        ga, gb, gc = ga[:n_s], gb[:n_s], gc[:n_s]
        ga_p = to_padded(ga, NEG_INF)
        gb_p = to_padded(gb, mode="edge")
        c0t = state_mlstm_c[l].astype(F32).swapaxes(-1, -2).reshape(db * hm, dh, dh)
        n0 = state_mlstm_n[l].astype(F32).reshape(db * hm, 1, dh)
        m0 = jnp.broadcast_to(state_mlstm_m[l].astype(F32).reshape(db * hm, 1, 1), (db * hm, 1, LANES))
        h_m, ct, n_m, m_m = _mlstm(to_padded(zm), ga_p, gb_p, head_rows(ga_p, 0, hm, db, L),
                                   c0t, n0, m0, p["g_m"], db, hm, L)
        s0 = state_gla[l].astype(F32).swapaxes(-1, -2).reshape(db, hg // 2, 2, dh, dkg)
        s0t = jnp.stack([jnp.pad(s0[:, :, 0], ((0, 0), (0, 0), (0, 0), (0, dkg))),
                         jnp.pad(s0[:, :, 1], ((0, 0), (0, 0), (0, 0), (dkg, 0)))], axis=2)
        h_g, st = _gla(to_padded(zg), to_padded(ga), p["wa"], p["ba"], s0t, p["g_g"], db, hg, L, ns)

        q_s = zq.reshape(ns, db, hf, dh).transpose(1, 2, 0, 3).reshape(db, hf * ns, dh)
        new_rows = lambda z: jnp.pad(z.reshape(ns, db, hf * dh).transpose(1, 0, 2).reshape(db, ns * hf, dh),
                                     ((0, 0), (0, LANES - ns * hf), (0, 0)))
        c_new = gb[:, 8:8 + hf].reshape(ns, db, hf)
        cq = jnp.broadcast_to(c_new.transpose(1, 2, 0).reshape(db, hf * ns, 1), (db, hf * ns, LANES))
        bn = jnp.pad(-c_new.transpose(1, 0, 2).reshape(db, 1, ns * hf), ((0, 0), (0, 0), (0, LANES - ns * hf)))
        o_f = _fox_sample(page_table, q_s, cq, new_rows(zkb), new_rows(zvb), bn, ck[l], cv[l], clf[l], hf, ns)
        h_f = o_f.reshape(db, hf, ns, dh).transpose(2, 0, 1, 3).reshape(n_s, hf * dh).astype(BF16)

        h, h_bf = _outproj_ln(from_padded(h_m), from_padded(h_g), h_f, p["w_out"], xs, p["ln1_g"], p["ln1_b"],
                              alpha, n_s)
        halo_s = _pad_conv_state(state_ffn_conv[l].astype(F32).transpose(1, 0, 2).reshape((CONV_W - 1) * db, 2 * f),
                                 f, fp)
        act, tg, tv = _ffn_up(h_bf, p["w_up"], p["conv_w"], p["conv_b"], halo_s, fp, n_s, 512, 1, db)
        xs, xs_bf = _ffn_down_ln(act, p["w_down"], h, p["ln2_g"], p["ln2_b"], alpha, n_s, fp // 4)

        conv_tail_s = lambda t: t[:, :f].reshape(CONV_W - 1, db, f).transpose(1, 0, 2)
        outs_s[0].append(ct.reshape(db, hm, dh, dh).swapaxes(-1, -2))
        outs_s[1].append(n_m.reshape(db, hm, dh))
        outs_s[2].append(m_m[:, 0, 0].reshape(db, hm))
        st_h = st.reshape(db, hg // 2, 2, dh, 2, dkg)
        st_h = jnp.stack([st_h[:, :, 0, :, 0, :], st_h[:, :, 1, :, 1, :]], axis=2)
        outs_s[3].append(st_h.reshape(db, hg, dh, dkg).swapaxes(-1, -2))
        outs_s[4].append(jnp.concatenate([conv_tail_s(tg), conv_tail_s(tv)], axis=-1))
        outs_s[5].append(zk.reshape(ns, db, hf, dh).transpose(1, 0, 2, 3))
        outs_s[6].append(zv.reshape(ns, db, hf, dh).transpose(1, 0, 2, 3))
        outs_s[7].append(gc[:, 8:8 + hf].reshape(ns, db, hf).transpose(1, 0, 2))

    y_p = xp.reshape(bp, seq, d)
    y_s = xs.reshape(ns, db, d).transpose(1, 0, 2)
    return (y_p, y_s) + tuple(jnp.stack(a) for a in outs_p) + tuple(jnp.stack(a) for a in outs_s)
```

```python
import functools

import numpy as np
import jax
import jax.numpy as jnp
from jax import lax
from jax.experimental import pallas as pl
from jax.experimental.pallas import tpu as pltpu

F32 = jnp.float32
BF16 = jnp.bfloat16

HEAD_DIM = 128
GLA_RANK = 16
GLA_TAU = 16.0
CONV_W = 3
LN_EPS = 1e-5
NORM_EPS = 1e-6

LANES = 128
SUBLANES = 8
VMEM_LIMIT = 56 * 1024 * 1024

CHUNK = 128
GATE_TILE = 256
ATTN_TILE = 512
ATTN_HEADS_PER_STEP = 4
FFN_ROW_TILE = 1024
FFN_COL_SLAB = 256
FFN_ROW_CHUNK = 512
NEG_INF = float("-inf")
LOG2E = 1.4426950408889634


def _params(*sem):
    return pltpu.CompilerParams(dimension_semantics=sem, vmem_limit_bytes=VMEM_LIMIT)


def _dot(a, b):
    return jnp.dot(a, b, preferred_element_type=F32)


def _dot_nt(a, b):
    return lax.dot_general(a, b, (((1,), (1,)), ((), ())), preferred_element_type=F32)


def _dot_tn(a, b):
    return lax.dot_general(a, b, (((0,), (0,)), ((), ())), preferred_element_type=F32)


def _log_sigmoid(x):
    return jnp.minimum(x, 0.0) - jnp.log1p(jnp.exp(-jnp.abs(x)))


def _sigmoid(x):
    return 1.0 / (1.0 + jnp.exp(-x))


def _split3(x):
    hi = x.astype(BF16)
    r = x - hi.astype(F32)
    mid = r.astype(BF16)
    lo = (r - mid.astype(F32)).astype(BF16)
    return hi, mid, lo


def _mm_kernel(x_ref, w_ref, *o_refs, scale):
    acc = _dot(x_ref[...], w_ref[...])
    if scale != 1.0:
        acc = acc * scale
    for o in o_refs:
        o[...] = acc.astype(o.dtype)


def _matmul(x, w, out_dtypes, tm, tn, name, scale=1.0):
    m, k = x.shape
    n = w.shape[1]
    tm = min(tm, m)
    tn = min(tn, n)
    assert m % tm == 0 and n % tn == 0
    return pl.pallas_call(
        functools.partial(_mm_kernel, scale=scale),
        grid=(m // tm, n // tn),
        in_specs=[pl.BlockSpec((tm, k), lambda i, j: (i, 0)),
                  pl.BlockSpec((k, tn), lambda i, j: (0, j))],
        out_specs=[pl.BlockSpec((tm, tn), lambda i, j: (i, j)) for _ in out_dtypes],
        out_shape=[jax.ShapeDtypeStruct((m, n), d) for d in out_dtypes],
        compiler_params=_params("parallel", "parallel"),
        name=name,
    )(x, w)


def _gates_kernel(x_ref, w_ref, ba_ref, bb_ref, tri_ref, oa_ref, ob_ref, oc_ref, carry, *, tiles_per_seq, n_ig):
    i = pl.program_id(0)
    tm = x_ref.shape[0]
    z = _dot(x_ref[...], w_ref[...])
    za = z[:, :LANES] + ba_ref[...]
    ls = _log_sigmoid(z[:, LANES:] + bb_ref[...])
    tri = tri_ref[...]
    hi, mid, lo = _split3(ls)
    cum = _dot(tri, hi) + _dot(tri, mid) + _dot(tri, lo)
    if tiles_per_seq > 1:
        @pl.when(i % tiles_per_seq == 0)
        def _():
            carry[...] = jnp.zeros_like(carry)
        cum = cum + carry[0:1, :]
        carry[...] = jnp.broadcast_to(cum[tm - 1:tm, :], carry.shape)
    lane = lax.broadcasted_iota(jnp.int32, (tm, LANES), 1)
    oa_ref[...] = jnp.where(lane < n_ig, za - cum, za)
    ob_ref[...] = cum
    oc_ref[...] = ls


def _gates(x, w_small, bias_a, bias_b, tri, tiles_per_seq, n_ig):
    n, d = x.shape
    tm = tri.shape[0]
    assert n % tm == 0
    out = jax.ShapeDtypeStruct((n, LANES), F32)
    return pl.pallas_call(
        functools.partial(_gates_kernel, tiles_per_seq=tiles_per_seq, n_ig=n_ig),
        grid=(n // tm,),
        in_specs=[pl.BlockSpec((tm, d), lambda i: (i, 0)),
                  pl.BlockSpec((d, 2 * LANES), lambda i: (0, 0)),
                  pl.BlockSpec((1, LANES), lambda i: (0, 0)),
                  pl.BlockSpec((1, LANES), lambda i: (0, 0)),
                  pl.BlockSpec((tm, tm), lambda i: (0, 0))],
        out_specs=[pl.BlockSpec((tm, LANES), lambda i: (i, 0))] * 3,
        out_shape=[out, out, out],
        scratch_shapes=[pltpu.VMEM((SUBLANES, LANES), F32)],
        compiler_params=_params("arbitrary"),
        name="gates",
    )(x, w_small, bias_a, bias_b, tri)


def _mlstm_kernel(q_ref, k_ref, v_ref, o_ref, ga_ref, gb_ref, arow_ref, c0_ref, n0_ref, m0_ref, g_ref,
                  h_ref, c_ref, n_ref, m_ref, ct_sc, n_sc, a_sc, *, n_heads):
    c = pl.program_id(1)
    L = q_ref.shape[0]
    dk = HEAD_DIM

    @pl.when(c == 0)
    def _():
        ct_sc[...] = c0_ref[...]
        n_sc[...] = jnp.broadcast_to(n0_ref[...], n_sc.shape)
        a_sc[...] = jnp.broadcast_to(m0_ref[...], a_sc.shape)

    lane = lax.broadcasted_iota(jnp.int32, (L, LANES), 1)
    ti = lax.broadcasted_iota(jnp.int32, (L, L), 0)
    si = lax.broadcasted_iota(jnp.int32, (L, L), 1)
    causal = si <= ti
    ga = ga_ref[...]
    gb = gb_ref[...]
    for hd in range(n_heads):
        hs = slice(hd * dk, (hd + 1) * dk)
        q = q_ref[:, hs]
        k = k_ref[:, hs]
        v = v_ref[:, hs]
        sel = lane == hd
        a_col = jnp.sum(jnp.where(sel, ga, 0.0), axis=1, keepdims=True)
        b_col = jnp.sum(jnp.where(sel, gb, 0.0), axis=1, keepdims=True)
        a_row = arow_ref[hd]
        a_prev = a_sc[hd, 0:1, 0:1]
        ct = ct_sc[hd]
        n_row = n_sc[hd, 0:1, :]
        mm = jnp.where(causal, a_row, NEG_INF)
        a_t = jnp.maximum(jnp.max(mm, axis=1, keepdims=True), a_prev)
        s = _dot_nt(q, k) * (dk ** -0.5) * jnp.exp(mm - a_t)
        inter = jnp.exp(a_prev - a_t)
        num = _dot(s.astype(BF16), v) + inter * _dot(q, ct.astype(BF16))
        den = (jnp.sum(s, axis=1, keepdims=True)
               + inter * jnp.sum(q.astype(F32) * n_row, axis=1, keepdims=True))
        h = num / jnp.maximum(jnp.abs(den), jnp.exp(-(b_col + a_t)))
        hm = _sigmoid(o_ref[:, hs].astype(F32)) * h
        hm = hm * lax.rsqrt(jnp.mean(hm * hm, axis=1, keepdims=True) + NORM_EPS) * g_ref[:, hs]
        h_ref[:, hs] = hm.astype(h_ref.dtype)

        a_end = jnp.maximum(jnp.max(a_row, axis=1, keepdims=True), a_prev)
        e_col = jnp.exp(a_col - a_end) * (dk ** -0.5)
        decay = jnp.exp(a_prev - a_end)
        ke = k.astype(F32) * e_col
        ct_new = decay * ct + _dot_tn(ke.astype(BF16), v)
        n_new = decay * n_row + jnp.sum(ke, axis=0, keepdims=True)
        ct_sc[hd] = ct_new
        n_sc[hd] = jnp.broadcast_to(n_new, n_sc.shape[1:])
        a_sc[hd] = jnp.broadcast_to(a_end, a_sc.shape[1:])

        @pl.when(c == pl.num_programs(1) - 1)
        def _():
            c_ref[hd] = ct_new
            n_ref[hd] = n_new
            m_ref[hd] = jnp.broadcast_to(b_col[L - 1:L, :] + a_end, m_ref.shape[1:])


def _mlstm(zm, ga, gb, a_rows, c0t, n0, m0, g_norm, batch, n_heads, seq):
    L = CHUNK
    nc = seq // L
    n = batch * seq
    dh = HEAD_DIM
    hw = n_heads * dh

    def col(off):
        return pl.BlockSpec((L, hw), lambda b, c: (b * nc + c, off))

    def rows():
        return pl.BlockSpec((L, LANES), lambda b, c: (b * nc + c, 0))

    def per_b(r, w):
        return pl.BlockSpec((None, n_heads, r, w), lambda b, c: (b, 0, 0, 0))

    return pl.pallas_call(
        functools.partial(_mlstm_kernel, n_heads=n_heads),
        grid=(batch, nc),
        in_specs=[col(0), col(1), col(2), col(3), rows(), rows(),
                  pl.BlockSpec((None, n_heads, 1, L), lambda b, c: (b, 0, 0, c)),
                  per_b(dh, dh), per_b(1, dh), per_b(1, LANES),
                  pl.BlockSpec((1, hw), lambda b, c: (0, 0))],
        out_specs=[pl.BlockSpec((L, hw), lambda b, c: (b * nc + c, 0)),
                   per_b(dh, dh), per_b(1, dh), per_b(1, LANES)],
        out_shape=[jax.ShapeDtypeStruct((n, hw), BF16),
                   jax.ShapeDtypeStruct((batch, n_heads, dh, dh), F32),
                   jax.ShapeDtypeStruct((batch, n_heads, 1, dh), F32),
                   jax.ShapeDtypeStruct((batch, n_heads, 1, LANES), F32)],
        scratch_shapes=[pltpu.VMEM((n_heads, dh, dh), F32), pltpu.VMEM((n_heads, SUBLANES, dh), F32),
                        pltpu.VMEM((n_heads, SUBLANES, LANES), F32)],
        compiler_params=_params("parallel", "arbitrary"),
        name="mlstm",
    )(zm, zm, zm, zm, ga, gb, a_rows, c0t, n0, m0, g_norm)


def _gla_levels(L):
    levels = []
    w = L // 2
    while w >= 1:
        levels.append(w)
        w //= 2
    return levels


def _gla_consts(L):
    t = np.arange(L)
    row, colj = t[:, None], t[None, :]
    mats = [(colj <= row), (colj > row)]
    masks = []
    for w in _gla_levels(L):
        mid = (t // (2 * w)) * 2 * w + w
        right = t >= mid
        mr = right[:, None] & (colj >= mid[:, None]) & (colj <= row)
        ml = (~right)[:, None] & (colj > row) & (colj < mid[:, None])
        mats.append(mr | ml)
        same = (t[:, None] // (2 * w)) == (t[None, :] // (2 * w))
        masks.append(same & right[:, None] & (~right)[None, :])
    masks.append(row == colj)
    m_all = np.concatenate([m.astype(np.float32) for m in mats], axis=0)
    return m_all, np.stack([m.astype(np.float32) for m in masks])


def _gla_kernel(q_ref, k_ref, v_ref, r_ref, ga_ref, wa_ref, ba_ref, mall_ref, mask_ref, s0_ref, g_ref,
                h_ref, s_ref, st_sc, *, n_valid):
    c = pl.program_id(1)
    L = q_ref.shape[0]
    dk = LANES // 2
    dv = HEAD_DIM
    n_lev = mask_ref.shape[0] - 1
    n_pair = st_sc.shape[0]

    @pl.when(c == 0)
    def _():
        st_sc[...] = s0_ref[...]

    ga = ga_ref[...].astype(BF16)
    mall = mall_ref[...]
    lane = lax.broadcasted_iota(jnp.int32, (1, LANES), 1)
    lm = [(lane < dk).astype(F32), (lane >= dk).astype(F32)]
    for p in range(n_pair):
        ps = slice(p * LANES, (p + 1) * LANES)
        q2 = q_ref[:, ps].astype(F32) * (dk ** -0.5)
        k2 = k_ref[:, ps].astype(F32)
        la = _log_sigmoid(_dot(ga, wa_ref[p]) + ba_ref[p]) * (1.0 / GLA_TAU)
        if n_valid < L:
            valid = lax.broadcasted_iota(jnp.int32, (L, LANES), 0) < n_valid
            la = jnp.where(valid, la, 0.0)
            k2 = jnp.where(valid, k2, 0.0)
        la_hi = la.astype(BF16)
        la_mid = (la - la_hi.astype(F32)).astype(BF16)
        e = jnp.exp(_dot(mall, la_hi) + _dot(mall, la_mid))

        q_in = q2 * e[0:L]
        k_end = (k2 * e[L:2 * L]).astype(BF16)
        decay = e[L - 1:L]
        kq = [(q2 * e[(2 + i) * L:(3 + i) * L], (k2 * e[(2 + i) * L:(3 + i) * L]).astype(BF16))
              for i in range(n_lev)]
        k2b = k2.astype(BF16)
        for hh in range(2):
            hs = slice((2 * p + hh) * dv, (2 * p + hh + 1) * dv)
            v = v_ref[:, hs]
            st = st_sc[p, hh]
            o = _dot_nt((q_in * lm[hh]).astype(BF16), st.astype(BF16))
            a = mask_ref[n_lev] * _dot_nt((q2 * lm[hh]).astype(BF16), k2b)
            for i in range(n_lev):
                qh, kh = kq[i]
                a = a + mask_ref[i] * _dot_nt((qh * lm[hh]).astype(BF16), kh)
            o = o + _dot(a.astype(BF16), v)
            st_sc[p, hh] = decay * st + _dot_tn(v, k_end)
            o = o * lax.rsqrt(jnp.mean(o * o, axis=1, keepdims=True) + NORM_EPS) * g_ref[:, hs]
            r = r_ref[:, hs].astype(F32)
            h_ref[:, hs] = (r * _sigmoid(r) * o).astype(h_ref.dtype)

    @pl.when(c == pl.num_programs(1) - 1)
    def _():
        s_ref[...] = st_sc[...]


def _gla(zg, ga, wa, ba, s0t, g_norm, batch, n_heads, seq, n_valid):
    L = CHUNK
    nc = seq // L
    n = batch * seq
    npair = n_heads // 2
    dv = HEAD_DIM
    m_all, masks = _gla_consts(L)
    m_all = jnp.asarray(m_all, BF16)
    masks = jnp.asarray(masks, F32)
    qkw = npair * LANES
    state = pl.BlockSpec((None, npair, 2, dv, LANES), lambda b, c: (b, 0, 0, 0, 0))

    return pl.pallas_call(
        functools.partial(_gla_kernel, n_valid=n_valid),
        grid=(batch, nc),
        in_specs=[pl.BlockSpec((L, qkw), lambda b, c: (b * nc + c, 0)),
                  pl.BlockSpec((L, qkw), lambda b, c: (b * nc + c, 1)),
                  pl.BlockSpec((L, 2 * qkw), lambda b, c: (b * nc + c, 1)),
                  pl.BlockSpec((L, 2 * qkw), lambda b, c: (b * nc + c, 2)),
                  pl.BlockSpec((L, LANES), lambda b, c: (b * nc + c, 0)),
                  pl.BlockSpec(wa.shape, lambda b, c: (0, 0, 0)),
                  pl.BlockSpec(ba.shape, lambda b, c: (0, 0, 0)),
                  pl.BlockSpec(m_all.shape, lambda b, c: (0, 0)),
                  pl.BlockSpec(masks.shape, lambda b, c: (0, 0, 0)),
                  state,
                  pl.BlockSpec((1, n_heads * dv), lambda b, c: (0, 0))],
        out_specs=[pl.BlockSpec((L, n_heads * dv), lambda b, c: (b * nc + c, 0)), state],
        out_shape=[jax.ShapeDtypeStruct((n, n_heads * dv), BF16),
                   jax.ShapeDtypeStruct((batch, npair, 2, dv, LANES), F32)],
        scratch_shapes=[pltpu.VMEM((npair, 2, dv, LANES), F32)],
        compiler_params=_params("parallel", "arbitrary"),
        name="gla",
    )(zg, zg, zg, zg, ga, wa, ba, m_all, masks, s0t, g_norm)


def _fox_prompt_kernel(qi_ref, ki_ref, q_ref, k_ref, v_ref, gb_ref, crow_ref, o_ref, m_sc, acc_sc, ccol_sc,
                       *, n_heads, lane0):
    g = pl.program_id(0)
    p = pl.program_id(1)
    qi = qi_ref[p]
    ki = ki_ref[p]
    tq = q_ref.shape[0]
    tk = k_ref.shape[0]
    dh = HEAD_DIM
    hpg = q_ref.shape[1] // dh
    groups_per_batch = n_heads // hpg

    @pl.when(ki == 0)
    def _():
        m_sc[...] = jnp.full_like(m_sc, NEG_INF)
        acc_sc[...] = jnp.zeros_like(acc_sc)
        lane = lax.broadcasted_iota(jnp.int32, (tq, LANES), 1)
        gb = gb_ref[...]
        for hh in range(hpg):
            hd = (g % groups_per_batch) * hpg + hh
            ccol_sc[hh] = LOG2E * jnp.sum(jnp.where(lane == lane0 + hd, gb, 0.0), axis=1, keepdims=True)

    def step(diagonal):
        for hh in range(hpg):
            hs = slice(hh * dh, (hh + 1) * dh)
            u = _dot_nt(q_ref[:, hs], k_ref[:, hs]) - LOG2E * crow_ref[hh]
            if diagonal:
                ti = lax.broadcasted_iota(jnp.int32, (tq, tk), 0)
                si = lax.broadcasted_iota(jnp.int32, (tq, tk), 1)
                u = jnp.where(si <= ti, u, NEG_INF)
            c2 = ccol_sc[hh]
            m_old = m_sc[hh]
            m_new = jnp.maximum(m_old, jnp.max(u, axis=1, keepdims=True) + c2)
            pr = jnp.exp2(u - (m_new - c2))
            v1 = jnp.concatenate([v_ref[:, hs], jnp.ones((tk, LANES), BF16)], axis=1)
            acc_sc[hh] = jnp.exp2(m_old - m_new) * acc_sc[hh] + _dot(pr.astype(BF16), v1)
            m_sc[hh] = m_new

    @pl.when(ki < qi)
    def _():
        step(False)

    @pl.when(ki == qi)
    def _():
        step(True)
        for hh in range(hpg):
            acc = acc_sc[hh]
            o_ref[:, hh * dh:(hh + 1) * dh] = (acc[:, :dh] / acc[:, dh:]).astype(o_ref.dtype)


def _fox_prompt(zq, zk, zv, gb, c_rows, batch, n_heads, seq, tile, lane0):
    n = batch * seq
    nq = seq // tile
    dh = HEAD_DIM
    hpg = ATTN_HEADS_PER_STEP
    gpb = n_heads // hpg
    pairs = [(a, b) for a in range(nq) for b in range(a + 1)]
    qi = jnp.asarray([a for a, _ in pairs], jnp.int32)
    ki = jnp.asarray([b for _, b in pairs], jnp.int32)
    c_rows = c_rows.reshape(batch * gpb, hpg, 1, seq)

    def qrow(g, p, qi, ki):
        return (g // gpb) * nq + qi[p]

    def krow(g, p, qi, ki):
        return (g // gpb) * nq + ki[p]

    grid_spec = pltpu.PrefetchScalarGridSpec(
        num_scalar_prefetch=2,
        grid=(batch * gpb, len(pairs)),
        in_specs=[pl.BlockSpec((tile, hpg * dh), lambda g, p, qi, ki: (qrow(g, p, qi, ki), g % gpb)),
                  pl.BlockSpec((tile, hpg * dh), lambda g, p, qi, ki: (krow(g, p, qi, ki), g % gpb)),
                  pl.BlockSpec((tile, hpg * dh), lambda g, p, qi, ki: (krow(g, p, qi, ki), g % gpb)),
                  pl.BlockSpec((tile, LANES), lambda g, p, qi, ki: (qrow(g, p, qi, ki), 0)),
                  pl.BlockSpec((None, hpg, 1, tile), lambda g, p, qi, ki: (g, 0, 0, ki[p]))],
        out_specs=pl.BlockSpec((tile, hpg * dh), lambda g, p, qi, ki: (qrow(g, p, qi, ki), g % gpb)),
        scratch_shapes=[pltpu.VMEM((hpg, tile, 1), F32), pltpu.VMEM((hpg, tile, dh + LANES), F32),
                        pltpu.VMEM((hpg, tile, 1), F32)],
    )
    return pl.pallas_call(
        functools.partial(_fox_prompt_kernel, n_heads=n_heads, lane0=lane0),
        grid_spec=grid_spec,
        out_shape=jax.ShapeDtypeStruct((n, n_heads * dh), BF16),
        compiler_params=_params("parallel", "arbitrary"),
        name="fox_prompt",
    )(qi, ki, zq, zk, zv, gb, c_rows)


def _fox_sample_kernel(pt_ref, q_ref, cq_ref, kn_ref, vn_ref, bn_ref, *rest, n_heads, n_new, group):
    k_refs, v_refs, lf_refs = rest[:group], rest[group:2 * group], rest[2 * group:3 * group]
    o_ref, m_sc, l_sc, acc_sc, carry_sc = rest[3 * group:]
    j = pl.program_id(1)
    nr = q_ref.shape[0]
    rows_pp = k_refs[0].shape[0]
    d = q_ref.shape[1]
    n_tiles = rows_pp // LANES
    scale = d ** -0.5

    @pl.when(j == 0)
    def _():
        m_sc[...] = jnp.full_like(m_sc, NEG_INF)
        l_sc[...] = jnp.zeros_like(l_sc)
        acc_sc[...] = jnp.zeros_like(acc_sc)
        carry_sc[...] = jnp.zeros_like(carry_sc)

    lane8 = lax.broadcasted_iota(jnp.int32, (SUBLANES, LANES), 1)
    row8 = lax.broadcasted_iota(jnp.int32, (SUBLANES, LANES), 0)

    def page_suffix(x):
        y = x
        z = x
        sh = n_heads
        while sh < LANES:
            y = y + jnp.where(lane8 + sh < LANES, pltpu.roll(y, LANES - sh, 1), 0.0)
            z = z + pltpu.roll(z, sh, 1)
            sh *= 2
        w = z
        sh = 1
        while sh < SUBLANES:
            w = w + jnp.where(row8 + sh < SUBLANES, pltpu.roll(w, SUBLANES - sh, 0), 0.0)
            sh *= 2
        return y - x + (w - z), jnp.broadcast_to(w[0:1, :], x.shape)

    q = q_ref[...]
    cq = cq_ref[...]
    rowi = lax.broadcasted_iota(jnp.int32, (nr, LANES), 0)
    lanei = lax.broadcasted_iota(jnp.int32, (nr, LANES), 1)
    head_ok = (rowi // n_new) == (lanei % n_heads)

    def update(tile_groups, vbs):
        flat = [t for ts in tile_groups for t in ts]
        m_old = m_sc[...]
        mx = flat[0]
        for t in flat[1:]:
            mx = jnp.maximum(mx, t)
        m_new = jnp.maximum(m_old, jnp.max(mx, axis=1, keepdims=True))
        alpha = jnp.exp(m_old - m_new)
        tot = None
        acc = alpha * acc_sc[...]
        for ts, vb in zip(tile_groups, vbs):
            ps = [jnp.exp(t - m_new) for t in ts]
            for t in ps:
                tot = t if tot is None else tot + t
            pcat = ps[0] if len(ps) == 1 else jnp.concatenate(ps, axis=1)
            acc = acc + _dot(pcat.astype(BF16), vb)
        l_sc[...] = alpha * l_sc[...] + jnp.sum(tot, axis=1, keepdims=True)
        acc_sc[...] = acc
        m_sc[...] = m_new

    carry = carry_sc[...]
    tile_groups = []
    for g in range(group):
        within, total = page_suffix(lf_refs[g][...])
        suffix = within + carry
        carry = carry + total
        s = _dot_nt(q, k_refs[g][...].astype(BF16)) * scale
        tile_groups.append([jnp.where(head_ok, s[:, r * LANES:(r + 1) * LANES] + cq + suffix[r:r + 1, :], NEG_INF)
                            for r in range(n_tiles)])
    carry_sc[...] = carry
    update(tile_groups, [v_refs[g][...].astype(BF16) for g in range(group)])

    @pl.when(j == pl.num_programs(1) - 1)
    def _():
        s_self = _dot_nt(q, kn_ref[...]) * scale + cq + bn_ref[...]
        ok = head_ok & (lanei < n_new * n_heads) & ((lanei // n_heads) <= (rowi % n_new))
        update([[jnp.where(ok, s_self, NEG_INF)]], [vn_ref[...]])
        o_ref[...] = acc_sc[...] / l_sc[...]


def _fox_sample(page_table, q, cq, kn, vn, bn, cache_k, cache_v, cache_lf, layer, n_heads, n_new):
    nb, n_pages = page_table.shape
    nr = q.shape[1]
    rows_pp = cache_k.shape[2]
    d = q.shape[2]
    group = max(g for g in (1, 2, 4, 8) if n_pages % g == 0)

    def page_spec(rows, width, g):
        return pl.BlockSpec((None, None, rows, width),
                            lambda b, j, pt: (layer, pt[b, n_pages - 1 - (j * group + g)], 0, 0))

    per_b = lambda rows, width: pl.BlockSpec((None, rows, width), lambda b, j, pt: (b, 0, 0))
    grid_spec = pltpu.PrefetchScalarGridSpec(
        num_scalar_prefetch=1,
        grid=(nb, n_pages // group),
        in_specs=([per_b(nr, d), per_b(nr, LANES), per_b(LANES, d), per_b(LANES, d), per_b(1, LANES)]
                  + [page_spec(rows_pp, d, g) for g in range(group)]
                  + [page_spec(rows_pp, d, g) for g in range(group)]
                  + [page_spec(SUBLANES, LANES, g) for g in range(group)]),
        out_specs=per_b(nr, d),
        scratch_shapes=[pltpu.VMEM((nr, 1), F32), pltpu.VMEM((nr, 1), F32), pltpu.VMEM((nr, d), F32),
                        pltpu.VMEM((SUBLANES, LANES), F32)],
    )
    return pl.pallas_call(
        functools.partial(_fox_sample_kernel, n_heads=n_heads, n_new=n_new, group=group),
        grid_spec=grid_spec,
        out_shape=jax.ShapeDtypeStruct((nb, nr, d), F32),
        compiler_params=_params("parallel", "arbitrary"),
        name="fox_sample",
    )(page_table, q, cq, kn, vn, bn, *([cache_k] * group), *([cache_v] * group), *([cache_lf] * group))


def _layer_norm(r, g, b):
    mu = jnp.mean(r, axis=1, keepdims=True)
    xc = r - mu
    var = jnp.mean(xc * xc, axis=1, keepdims=True)
    return xc * lax.rsqrt(var + LN_EPS) * g + b


def _outproj_kernel(hm_ref, hg_ref, hf_ref, wm_ref, wg_ref, wf_ref, x_ref, g_ref, b_ref, h_ref, hb_ref, *, alpha):
    mix = _dot(hm_ref[...], wm_ref[...]) + _dot(hg_ref[...], wg_ref[...]) + _dot(hf_ref[...], wf_ref[...])
    h = _layer_norm(alpha * x_ref[...] + mix, g_ref[...], b_ref[...])
    h_ref[...] = h
    hb_ref[...] = h.astype(BF16)


def _outproj_ln(hm, hg, hf, w_out, x, g, b, alpha, tm):
    n, d = x.shape
    tm = min(tm, n)
    wm, wg, wf = hm.shape[1], hg.shape[1], hf.shape[1]
    assert wm == wg and wf == wm + wg and n % tm == 0
    const = pl.Buffered(1)
    return pl.pallas_call(
        functools.partial(_outproj_kernel, alpha=alpha),
        grid=(n // tm,),
        in_specs=[pl.BlockSpec((tm, wm), lambda i: (i, 0)),
                  pl.BlockSpec((tm, wg), lambda i: (i, 0)),
                  pl.BlockSpec((tm, wf), lambda i: (i, 0)),
                  pl.BlockSpec((wm, d), lambda i: (0, 0), pipeline_mode=const),
                  pl.BlockSpec((wg, d), lambda i: (1, 0), pipeline_mode=const),
                  pl.BlockSpec((wf, d), lambda i: (1, 0), pipeline_mode=const),
                  pl.BlockSpec((tm, d), lambda i: (i, 0)),
                  pl.BlockSpec((1, d), lambda i: (0, 0)),
                  pl.BlockSpec((1, d), lambda i: (0, 0))],
        out_specs=[pl.BlockSpec((tm, d), lambda i: (i, 0))] * 2,
        out_shape=[jax.ShapeDtypeStruct((n, d), F32), jax.ShapeDtypeStruct((n, d), BF16)],
        compiler_params=_params("parallel"),
        name="outproj_ln",
    )(hm, hg, hf, w_out, w_out, w_out, x, g, b)


def _ffn_up_kernel(x_ref, wg_ref, wv_ref, cwg_ref, cwv_ref, cbg_ref, cbv_ref, hg0_ref, hv0_ref,
                   o_ref, tg_ref, tv_ref, halo_g, halo_v, *, tiles_per_seq, shift):
    i = pl.program_id(0)
    j = pl.program_id(1)
    tm = x_ref.shape[0]
    hr = hg0_ref.shape[0]
    tn = o_ref.shape[1]

    @pl.when(i % tiles_per_seq == 0)
    def _():
        halo_g[j] = hg0_ref[...]
        halo_v[j] = hv0_ref[...]

    rc = min(FFN_ROW_CHUNK, tm)

    def conv(cs, cw_ref, cb_ref, prev, u):
        ext = jnp.concatenate([prev, u], axis=0)
        cw = cw_ref[:, cs]
        return (cb_ref[:, cs] + cw[0:1] * ext[hr - 2 * shift:hr - 2 * shift + rc]
                + cw[1:2] * ext[hr - shift:hr - shift + rc] + cw[2:3] * u)

    def finish(rs, cs, prev_g, prev_v, ug, uv):
        yg = conv(cs, cwg_ref, cbg_ref, prev_g, ug)
        yv = conv(cs, cwv_ref, cbv_ref, prev_v, uv)
        o_ref[rs, cs] = (yg * _sigmoid(yg) * yv).astype(o_ref.dtype)

    halo_g0 = halo_g[j]
    halo_v0 = halo_v[j]
    tails_g, tails_v = [], []
    pending = None
    for c0 in range(0, tn, FFN_COL_SLAB):
        cs = slice(c0, c0 + FFN_COL_SLAB)
        prev_g, prev_v = halo_g0[:, cs], halo_v0[:, cs]
        for r0 in range(0, tm, rc):
            rs = slice(r0, r0 + rc)
            ug = _dot(x_ref[rs, :], wg_ref[:, cs])
            uv = _dot(x_ref[rs, :], wv_ref[:, cs])
            if pending is not None:
                finish(*pending)
            pending = (rs, cs, prev_g, prev_v, ug, uv)
            prev_g, prev_v = ug[rc - hr:], uv[rc - hr:]
        tails_g.append(prev_g)
        tails_v.append(prev_v)
    finish(*pending)
    tail_g = jnp.concatenate(tails_g, axis=1)
    tail_v = jnp.concatenate(tails_v, axis=1)
    tg_ref[...] = tail_g
    tv_ref[...] = tail_v
    halo_g[j] = tail_g
    halo_v[j] = tail_v


def _ffn_up(x, w_up, conv_w, conv_b, halo0, fp, tm, tn, tiles_per_seq, shift):
    n, d = x.shape
    tm = min(tm, n)
    nj = fp // tn
    ni = n // tm
    hr = halo0.shape[0] // (ni // tiles_per_seq)
    assert n % tm == 0 and fp % tn == 0 and hr >= 2 * shift
    seq = lambda i: i // tiles_per_seq
    return pl.pallas_call(
        functools.partial(_ffn_up_kernel, tiles_per_seq=tiles_per_seq, shift=shift),
        grid=(ni, nj),
        in_specs=[pl.BlockSpec((tm, d), lambda i, j: (i, 0)),
                  pl.BlockSpec((d, tn), lambda i, j: (0, j)),
                  pl.BlockSpec((d, tn), lambda i, j: (0, nj + j)),
                  pl.BlockSpec((CONV_W, tn), lambda i, j: (0, j)),
                  pl.BlockSpec((CONV_W, tn), lambda i, j: (0, nj + j)),
                  pl.BlockSpec((1, tn), lambda i, j: (0, j)),
                  pl.BlockSpec((1, tn), lambda i, j: (0, nj + j)),
                  pl.BlockSpec((hr, tn), lambda i, j: (seq(i), j)),
                  pl.BlockSpec((hr, tn), lambda i, j: (seq(i), nj + j))],
        out_specs=[pl.BlockSpec((tm, tn), lambda i, j: (i, j)),
                   pl.BlockSpec((hr, tn), lambda i, j: (i, j)),
                   pl.BlockSpec((hr, tn), lambda i, j: (i, j))],
        out_shape=[jax.ShapeDtypeStruct((n, fp), BF16),
                   jax.ShapeDtypeStruct((ni * hr, fp), F32),
                   jax.ShapeDtypeStruct((ni * hr, fp), F32)],
        scratch_shapes=[pltpu.VMEM((nj, hr, tn), F32), pltpu.VMEM((nj, hr, tn), F32)],
        compiler_params=_params("arbitrary", "arbitrary"),
        name="ffn_up",
    )(x, w_up, w_up, conv_w, conv_w, conv_b, conv_b, halo0, halo0)


def _ffn_down_kernel(a_ref, w_ref, h_ref, g_ref, b_ref, x_ref, xb_ref, acc, *, alpha):
    k = pl.program_id(1)

    @pl.when(k == 0)
    def _():
        acc[...] = jnp.zeros_like(acc)

    acc[...] += _dot(a_ref[...], w_ref[...])

    @pl.when(k == pl.num_programs(1) - 1)
    def _():
        x = _layer_norm(alpha * h_ref[...] + acc[...], g_ref[...], b_ref[...])
        x_ref[...] = x
        xb_ref[...] = x.astype(BF16)


def _ffn_down_ln(a, w_down, h, g, b, alpha, tm, tk):
    n, fp = a.shape
    d = w_down.shape[1]
    tm = min(tm, n)
    assert n % tm == 0 and fp % tk == 0
    return pl.pallas_call(
        functools.partial(_ffn_down_kernel, alpha=alpha),
        grid=(n // tm, fp // tk),
        in_specs=[pl.BlockSpec((tm, tk), lambda i, k: (i, k)),
                  pl.BlockSpec((tk, d), lambda i, k: (k, 0)),
                  pl.BlockSpec((tm, d), lambda i, k: (i, 0)),
                  pl.BlockSpec((1, d), lambda i, k: (0, 0)),
                  pl.BlockSpec((1, d), lambda i, k: (0, 0))],
        out_specs=[pl.BlockSpec((tm, d), lambda i, k: (i, 0))] * 2,
        out_shape=[jax.ShapeDtypeStruct((n, d), F32), jax.ShapeDtypeStruct((n, d), BF16)],
        scratch_shapes=[pltpu.VMEM((tm, d), F32)],
        compiler_params=_params("parallel", "arbitrary"),
        name="ffn_down_ln",
    )(a, w_down, h, g, b)


def _prep_layer(l, dims, w_in, b_m_ig, b_m_fg, w_g_alpha_up, b_g_alpha, b_f, g_m_norm, g_g_norm, w_out,
                ln1_g, ln1_b, w_up, conv_w, conv_b, w_down, ln2_g, ln2_b):
    d, hm, hg, hf, f, fp = dims
    dh = HEAD_DIM
    dkg = dh // 2
    sizes = [hm * dh] * 4 + [hm, hm] + [hg * dkg] * 2 + [hg * dh] * 2 + [GLA_RANK] + [hf * dh] * 3 + [hf]
    offs = np.concatenate([[0], np.cumsum(sizes)]).tolist()
    (o_mq, o_mk, o_mv, o_mo, o_mi, o_mf, o_gq, o_gk, o_gv, o_gr, o_ga, o_fq, o_fk, o_fv, o_ff, _) = offs
    w = w_in[l]
    col = lambda a, b: w[:, a:b]
    zeros = lambda c: jnp.zeros((d, c), F32)
    blk_a = jnp.concatenate([col(o_mi, o_mi + hm), zeros(8 - hm), col(o_ga, o_ga + GLA_RANK),
                             zeros(LANES - 8 - GLA_RANK)], axis=1)
    blk_b = jnp.concatenate([col(o_mf, o_mf + hm), zeros(8 - hm), col(o_ff, o_ff + hf),
                             zeros(LANES - 8 - hf)], axis=1)
    pad1 = lambda v, lo, total: jnp.pad(v, (lo, total - lo - v.shape[0]))[None, :]
    wa = jnp.zeros((LANES, hg * dkg), F32).at[8:8 + GLA_RANK].set(w_g_alpha_up[l])
    npair = hg // 2
    pad_f = lambda a: jnp.pad(a, [(0, 0)] * (a.ndim - 1) + [(0, fp - f)])
    return dict(
        w_m=col(o_mq, o_mi).astype(BF16),
        w_g=col(o_gq, o_ga).astype(BF16),
        w_fq=col(o_fq, o_fk).astype(BF16),
        w_fk=col(o_fk, o_fv).astype(BF16),
        w_fv=col(o_fv, o_ff).astype(BF16),
        w_small=jnp.concatenate([blk_a, blk_b], axis=1).astype(BF16),
        bias_a=pad1(b_m_ig[l], 0, LANES),
        bias_b=pad1(b_m_fg[l], 0, LANES) + pad1(b_f[l], 8, LANES),
        wa=wa.reshape(LANES, npair, LANES).transpose(1, 0, 2).astype(BF16),
        ba=b_g_alpha[l].reshape(npair, 1, LANES),
        g_m=g_m_norm[l][None, :], g_g=g_g_norm[l][None, :],
        w_out=w_out[l].astype(BF16),
        ln1_g=ln1_g[l][None, :], ln1_b=ln1_b[l][None, :], ln2_g=ln2_g[l][None, :], ln2_b=ln2_b[l][None, :],
        w_up=jnp.concatenate([pad_f(w_up[l][:, :f]), pad_f(w_up[l][:, f:])], axis=1).astype(BF16),
        conv_w=jnp.concatenate([pad_f(conv_w[l][:, :f]), pad_f(conv_w[l][:, f:])], axis=1),
        conv_b=jnp.concatenate([pad_f(conv_b[l][:f]), pad_f(conv_b[l][f:])])[None, :],
        w_down=jnp.pad(w_down[l], ((0, fp - f), (0, 0))).astype(BF16),
    )


def _pad_conv_state(s, f, fp):
    pad = [(0, 0)] * (s.ndim - 1) + [(0, fp - f)]
    return jnp.concatenate([jnp.pad(s[..., :f], pad), jnp.pad(s[..., f:], pad)], axis=-1)


def _mixer_rows(x_bf, p, tm, q_scale=1.0):
    zm, = _matmul(x_bf, p["w_m"], [BF16], tm, 512, "proj_m")
    zg, = _matmul(x_bf, p["w_g"], [BF16], tm, 512, "proj_g")
    zq, = _matmul(x_bf, p["w_fq"], [BF16], tm, 512, "proj_fq", scale=q_scale)
    zk, zkb = _matmul(x_bf, p["w_fk"], [F32, BF16], tm, 512, "proj_fk")
    zv, zvb = _matmul(x_bf, p["w_fv"], [F32, BF16], tm, 512, "proj_fv")
    return zm, zg, zq, zk, zkb, zv, zvb


def kernel(x_prompt, x_sample, state_mlstm_c, state_mlstm_n, state_mlstm_m, state_gla, state_ffn_conv,
           cache_k, cache_v, cache_logf, page_table,
           w_in, b_m_ig, b_m_fg, w_g_alpha_up, b_g_alpha, b_f, g_m_norm, g_g_norm, w_out,
           ln1_g, ln1_b, w_up, conv_w, conv_b, w_down, ln2_g, ln2_b):
    bp, seq, d = x_prompt.shape
    db, ns, _ = x_sample.shape
    depth = w_in.shape[0]
    hm = b_m_ig.shape[1]
    hf = b_f.shape[1]
    dh = HEAD_DIM
    hg = g_g_norm.shape[1] // dh
    dkg = dh // 2
    f = w_down.shape[1]
    fp = -(-f // 512) * 512
    alpha = (2.0 * depth) ** 0.25
    n_pool, page = cache_k.shape[1], cache_k.shape[2]
    assert hm <= 8 and hf <= 8 and hg % 2 == 0 and seq % 512 == 0 and ns <= CHUNK
    assert page * hf == SUBLANES * LANES and ns * hf <= LANES
    dims = (d, hm, hg, hf, f, fp)
    L = CHUNK
    n_p = bp * seq

    tri_p = jnp.asarray(np.tril(np.ones((GATE_TILE, GATE_TILE), np.float32)), BF16)
    gs_rows = LANES
    r = np.arange(gs_rows)
    tri_s_np = ((r[:, None] % db == r[None, :] % db) & (r[None, :] // db <= r[:, None] // db)
                & (r[:, None] < ns * db) & (r[None, :] < ns * db))
    tri_s = jnp.asarray(tri_s_np.astype(np.float32), BF16)

    xp = x_prompt.astype(F32).reshape(n_p, d)
    xp_bf = xp.astype(BF16)
    xs = x_sample.astype(F32).transpose(1, 0, 2).reshape(ns * db, d)
    xs_bf = xs.astype(BF16)

    ck = cache_k.astype(F32).reshape(depth, n_pool, page * hf, dh)
    cv = cache_v.astype(F32).reshape(depth, n_pool, page * hf, dh)
    clf = cache_logf.astype(F32).reshape(depth, n_pool, SUBLANES, LANES)

    def to_padded(z, pad_value=0.0, mode="constant"):
        c = z.shape[-1]
        a = z.reshape(ns, db, c).transpose(1, 0, 2)
        if mode == "edge":
            a = jnp.pad(a, ((0, 0), (0, L - ns), (0, 0)), mode="edge")
        else:
            a = jnp.pad(a, ((0, 0), (0, L - ns), (0, 0)), constant_values=pad_value)
        return a.reshape(db * L, c)

    def from_padded(y):
        c = y.shape[-1]
        return y.reshape(db, L, c)[:, :ns].transpose(1, 0, 2).reshape(ns * db, c)

    def head_rows(g, lane0, nh, batch, t):
        return g[:, lane0:lane0 + nh].reshape(batch, t, nh).transpose(0, 2, 1).reshape(batch * nh, 1, t)

    outs_p = [[] for _ in range(8)]
    outs_s = [[] for _ in range(8)]
    for l in range(depth):
        p = _prep_layer(l, dims, w_in, b_m_ig, b_m_fg, w_g_alpha_up, b_g_alpha, b_f, g_m_norm, g_g_norm, w_out,
                        ln1_g, ln1_b, w_up, conv_w, conv_b, w_down, ln2_g, ln2_b)

        zm, zg, zq, zk, zkb, zv, zvb = _mixer_rows(xp_bf, p, 1024, q_scale=LOG2E * dh ** -0.5)
        ga, gb, gc = _gates(xp_bf, p["w_small"], p["bias_a"], p["bias_b"], tri_p, seq // GATE_TILE, hm)
        zeros = lambda *s: jnp.zeros(s, F32)
        h_m, ct, n_m, m_m = _mlstm(zm, ga, gb, head_rows(ga, 0, hm, bp, seq).reshape(bp, hm, 1, seq),
                                   zeros(bp, hm, dh, dh), zeros(bp, hm, 1, dh), zeros(bp, hm, 1, LANES),
                                   p["g_m"], bp, hm, seq)
        h_g, st = _gla(zg, ga, p["wa"], p["ba"], zeros(bp, hg // 2, 2, dh, LANES), p["g_g"], bp, hg, seq, L)
        h_f = _fox_prompt(zq, zkb, zvb, gb, head_rows(gb, 8, hf, bp, seq), bp, hf, seq, ATTN_TILE, 8)
        h, h_bf = _outproj_ln(h_m, h_g, h_f, p["w_out"], xp, p["ln1_g"], p["ln1_b"], alpha, 512)
        tm_f = min(FFN_ROW_TILE, seq)
        act, tg, tv = _ffn_up(h_bf, p["w_up"], p["conv_w"], p["conv_b"], zeros(bp * SUBLANES, 2 * fp),
                              fp, tm_f, 512, seq // tm_f, 1)
        xp, xp_bf = _ffn_down_ln(act, p["w_down"], h, p["ln2_g"], p["ln2_b"], alpha, 512, fp // 4)

        def conv_tail_p(t):
            t = t.reshape(bp, seq // tm_f, SUBLANES, fp)[:, -1, SUBLANES - (CONV_W - 1):, :f]
            return t
        outs_p[0].append(ct.reshape(bp, hm, dh, dh).swapaxes(-1, -2))
        outs_p[1].append(n_m.reshape(bp, hm, dh))
        outs_p[2].append(m_m[:, :, 0, 0])
        st_h = st.reshape(bp, hg // 2, 2, dh, 2, dkg)
        st_h = jnp.stack([st_h[:, :, 0, :, 0, :], st_h[:, :, 1, :, 1, :]], axis=2)
        outs_p[3].append(st_h.reshape(bp, hg, dh, dkg).swapaxes(-1, -2))
        outs_p[4].append(jnp.concatenate([conv_tail_p(tg), conv_tail_p(tv)], axis=-1))
        outs_p[5].append(zk.reshape(bp, seq, hf, dh))
        outs_p[6].append(zv.reshape(bp, seq, hf, dh))
        outs_p[7].append(gc[:, 8:8 + hf].reshape(bp, seq, hf))

        n_s = ns * db
        zm, zg, zq, zk, zkb, zv, zvb = _mixer_rows(xs_bf, p, n_s)
        xs_pad = jnp.pad(xs_bf, ((0, gs_rows - n_s), (0, 0)))
        ga, gb, gc = _gates(xs_pad, p["w_small"], p["bias_a"], p["bias_b"], tri_s, 1, hm)
        ga, gb, gc = ga[:n_s], gb[:n_s], gc[:n_s]
        ga_p = to_padded(ga, NEG_INF)
        gb_p = to_padded(gb, mode="edge")
        c0t = state_mlstm_c[l].astype(F32).swapaxes(-1, -2)
        n0 = state_mlstm_n[l].astype(F32).reshape(db, hm, 1, dh)
        m0 = jnp.broadcast_to(state_mlstm_m[l].astype(F32).reshape(db, hm, 1, 1), (db, hm, 1, LANES))
        h_m, ct, n_m, m_m = _mlstm(to_padded(zm), ga_p, gb_p, head_rows(ga_p, 0, hm, db, L).reshape(db, hm, 1, L),
                                   c0t, n0, m0, p["g_m"], db, hm, L)
        s0 = state_gla[l].astype(F32).swapaxes(-1, -2).reshape(db, hg // 2, 2, dh, dkg)
        s0t = jnp.stack([jnp.pad(s0[:, :, 0], ((0, 0), (0, 0), (0, 0), (0, dkg))),
                         jnp.pad(s0[:, :, 1], ((0, 0), (0, 0), (0, 0), (dkg, 0)))], axis=2)
        h_g, st = _gla(to_padded(zg), to_padded(ga), p["wa"], p["ba"], s0t, p["g_g"], db, hg, L, ns)

        q_s = zq.reshape(ns, db, hf, dh).transpose(1, 2, 0, 3).reshape(db, hf * ns, dh)
        new_rows = lambda z: jnp.pad(z.reshape(ns, db, hf * dh).transpose(1, 0, 2).reshape(db, ns * hf, dh),
                                     ((0, 0), (0, LANES - ns * hf), (0, 0)))
        c_new = gb[:, 8:8 + hf].reshape(ns, db, hf)
        cq = jnp.broadcast_to(c_new.transpose(1, 2, 0).reshape(db, hf * ns, 1), (db, hf * ns, LANES))
        bn = jnp.pad(-c_new.transpose(1, 0, 2).reshape(db, 1, ns * hf), ((0, 0), (0, 0), (0, LANES - ns * hf)))
        o_f = _fox_sample(page_table, q_s, cq, new_rows(zkb), new_rows(zvb), bn, ck, cv, clf, l, hf, ns)
        h_f = o_f.reshape(db, hf, ns, dh).transpose(2, 0, 1, 3).reshape(n_s, hf * dh).astype(BF16)

        h, h_bf = _outproj_ln(from_padded(h_m), from_padded(h_g), h_f, p["w_out"], xs, p["ln1_g"], p["ln1_b"],
                              alpha, n_s)
        halo_s = _pad_conv_state(state_ffn_conv[l].astype(F32).transpose(1, 0, 2).reshape((CONV_W - 1) * db, 2 * f),
                                 f, fp)
        act, tg, tv = _ffn_up(h_bf, p["w_up"], p["conv_w"], p["conv_b"], halo_s, fp, n_s, 512, 1, db)
        xs, xs_bf = _ffn_down_ln(act, p["w_down"], h, p["ln2_g"], p["ln2_b"], alpha, n_s, fp // 4)

        conv_tail_s = lambda t: t[:, :f].reshape(CONV_W - 1, db, f).transpose(1, 0, 2)
        outs_s[0].append(ct.reshape(db, hm, dh, dh).swapaxes(-1, -2))
        outs_s[1].append(n_m.reshape(db, hm, dh))
        outs_s[2].append(m_m[:, :, 0, 0])
        st_h = st.reshape(db, hg // 2, 2, dh, 2, dkg)
        st_h = jnp.stack([st_h[:, :, 0, :, 0, :], st_h[:, :, 1, :, 1, :]], axis=2)
        outs_s[3].append(st_h.reshape(db, hg, dh, dkg).swapaxes(-1, -2))
        outs_s[4].append(jnp.concatenate([conv_tail_s(tg), conv_tail_s(tv)], axis=-1))
        outs_s[5].append(zk.reshape(ns, db, hf, dh).transpose(1, 0, 2, 3))
        outs_s[6].append(zv.reshape(ns, db, hf, dh).transpose(1, 0, 2, 3))
        outs_s[7].append(gc[:, 8:8 + hf].reshape(ns, db, hf).transpose(1, 0, 2))

    y_p = xp.reshape(bp, seq, d)
    y_s = xs.reshape(ns, db, d).transpose(1, 0, 2)
    return (y_p, y_s) + tuple(jnp.stack(a) for a in outs_p) + tuple(jnp.stack(a) for a in outs_s)
```

```python
import functools

import numpy as np
import jax
import jax.numpy as jnp
from jax import lax
from jax.experimental import pallas as pl
from jax.experimental.pallas import tpu as pltpu

F32 = jnp.float32
BF16 = jnp.bfloat16

HEAD_DIM = 128
GLA_RANK = 16
GLA_TAU = 16.0
CONV_W = 3
LN_EPS = 1e-5
NORM_EPS = 1e-6

LANES = 128
SUBLANES = 8
VMEM_LIMIT = 56 * 1024 * 1024

CHUNK = 128
GATE_TILE = 256
ATTN_TILE = 512
ATTN_HEADS_PER_STEP = 4
FFN_ROW_TILE = 1024
FFN_COL_SLAB = 256
FFN_ROW_CHUNK = 512
FFN_EPILOGUE_ROWS = 512
FFN_DOWN_ROW_TILE = 256
NEG_INF = float("-inf")
LOG2E = 1.4426950408889634


def _params(*sem, flags=None):
    return pltpu.CompilerParams(dimension_semantics=sem, vmem_limit_bytes=VMEM_LIMIT, flags=flags)


def _dot(a, b):
    return jnp.dot(a, b, preferred_element_type=F32)


def _dot_nt(a, b):
    return lax.dot_general(a, b, (((1,), (1,)), ((), ())), preferred_element_type=F32)


def _dot_tn(a, b):
    return lax.dot_general(a, b, (((0,), (0,)), ((), ())), preferred_element_type=F32)


def _log_sigmoid(x):
    return jnp.minimum(x, 0.0) - jnp.log1p(jnp.exp(-jnp.abs(x)))


def _sigmoid(x):
    return 1.0 / (1.0 + jnp.exp(-x))


def _split3(x):
    hi = x.astype(BF16)
    r = x - hi.astype(F32)
    mid = r.astype(BF16)
    lo = (r - mid.astype(F32)).astype(BF16)
    return hi, mid, lo


def _mm_kernel(x_ref, w_ref, *o_refs, scale):
    acc = _dot(x_ref[...], w_ref[...])
    if scale != 1.0:
        acc = acc * scale
    for o in o_refs:
        o[...] = acc.astype(o.dtype)


def _matmul(x, w, out_dtypes, tm, tn, name, scale=1.0):
    m, k = x.shape
    n = w.shape[1]
    tm = min(tm, m)
    tn = min(tn, n)
    assert m % tm == 0 and n % tn == 0
    return pl.pallas_call(
        functools.partial(_mm_kernel, scale=scale),
        grid=(m // tm, n // tn),
        in_specs=[pl.BlockSpec((tm, k), lambda i, j: (i, 0)),
                  pl.BlockSpec((k, tn), lambda i, j: (0, j))],
        out_specs=[pl.BlockSpec((tm, tn), lambda i, j: (i, j)) for _ in out_dtypes],
        out_shape=[jax.ShapeDtypeStruct((m, n), d) for d in out_dtypes],
        compiler_params=_params("parallel", "parallel"),
        name=name,
    )(x, w)


def _cast_pad_kernel(x_ref, o_ref, *, blocks_per_part, valid_per_part):
    c = pl.program_id(0)
    is_pad = (c % blocks_per_part) >= valid_per_part
    o_ref[...] = jnp.where(is_pad, 0.0, x_ref[...]).astype(o_ref.dtype)


def _cast_pad(w, layer, axis, n_parts, part, part_padded):
    _, r, c = w.shape
    valid = part // LANES
    per = part_padded // LANES
    assert part % LANES == 0 and part_padded % LANES == 0

    def src(blk):
        return (blk // per) * valid + jnp.minimum(blk % per, valid - 1)

    if axis == 2:
        in_spec = pl.BlockSpec((None, r, LANES), lambda blk: (layer, 0, src(blk)))
        out_spec = pl.BlockSpec((r, LANES), lambda blk: (0, blk))
        out_shape = (r, n_parts * part_padded)
    else:
        in_spec = pl.BlockSpec((None, LANES, c), lambda blk: (layer, src(blk), 0))
        out_spec = pl.BlockSpec((LANES, c), lambda blk: (blk, 0))
        out_shape = (n_parts * part_padded, c)
    return pl.pallas_call(
        functools.partial(_cast_pad_kernel, blocks_per_part=per, valid_per_part=valid),
        grid=(n_parts * per,),
        in_specs=[in_spec],
        out_specs=out_spec,
        out_shape=jax.ShapeDtypeStruct(out_shape, BF16),
        compiler_params=_params("parallel"),
        name="cast_pad",
    )(w)


def _gates_kernel(x_ref, w_ref, ba_ref, bb_ref, tri_ref, oa_ref, ob_ref, oc_ref, carry, *, tiles_per_seq, n_ig):
    i = pl.program_id(0)
    tm = x_ref.shape[0]
    z = _dot(x_ref[...], w_ref[...])
    za = z[:, :LANES] + ba_ref[...]
    ls = _log_sigmoid(z[:, LANES:] + bb_ref[...])
    tri = tri_ref[...]
    hi, mid, lo = _split3(ls)
    cum = _dot(tri, hi) + _dot(tri, mid) + _dot(tri, lo)
    if tiles_per_seq > 1:
        @pl.when(i % tiles_per_seq == 0)
        def _():
            carry[...] = jnp.zeros_like(carry)
        cum = cum + carry[0:1, :]
        carry[...] = jnp.broadcast_to(cum[tm - 1:tm, :], carry.shape)
    lane = lax.broadcasted_iota(jnp.int32, (tm, LANES), 1)
    oa_ref[...] = jnp.where(lane < n_ig, za - cum, za)
    ob_ref[...] = cum
    oc_ref[...] = ls


def _gates(x, w_small, bias_a, bias_b, tri, tiles_per_seq, n_ig):
    n, d = x.shape
    tm = tri.shape[0]
    assert n % tm == 0
    out = jax.ShapeDtypeStruct((n, LANES), F32)
    return pl.pallas_call(
        functools.partial(_gates_kernel, tiles_per_seq=tiles_per_seq, n_ig=n_ig),
        grid=(n // tm,),
        in_specs=[pl.BlockSpec((tm, d), lambda i: (i, 0)),
                  pl.BlockSpec((d, 2 * LANES), lambda i: (0, 0)),
                  pl.BlockSpec((1, LANES), lambda i: (0, 0)),
                  pl.BlockSpec((1, LANES), lambda i: (0, 0)),
                  pl.BlockSpec((tm, tm), lambda i: (0, 0))],
        out_specs=[pl.BlockSpec((tm, LANES), lambda i: (i, 0))] * 3,
        out_shape=[out, out, out],
        scratch_shapes=[pltpu.VMEM((SUBLANES, LANES), F32)],
        compiler_params=_params("arbitrary"),
        name="gates",
    )(x, w_small, bias_a, bias_b, tri)


def _mlstm_kernel(q_ref, k_ref, v_ref, o_ref, ga_ref, gb_ref, arow_ref, c0_ref, n0_ref, m0_ref, g_ref,
                  h_ref, c_ref, n_ref, m_ref, ct_sc, n_sc, a_sc, *, n_heads):
    c = pl.program_id(1)
    L = q_ref.shape[0]
    dk = HEAD_DIM

    @pl.when(c == 0)
    def _():
        ct_sc[...] = c0_ref[...]
        n_sc[...] = jnp.broadcast_to(n0_ref[...], n_sc.shape)
        a_sc[...] = jnp.broadcast_to(m0_ref[...], a_sc.shape)

    lane = lax.broadcasted_iota(jnp.int32, (L, LANES), 1)
    ti = lax.broadcasted_iota(jnp.int32, (L, L), 0)
    si = lax.broadcasted_iota(jnp.int32, (L, L), 1)
    causal = si <= ti
    ga = ga_ref[...]
    gb = gb_ref[...]
    for hd in range(n_heads):
        hs = slice(hd * dk, (hd + 1) * dk)
        q = q_ref[:, hs]
        k = k_ref[:, hs]
        v = v_ref[:, hs]
        sel = lane == hd
        a_col = jnp.sum(jnp.where(sel, ga, 0.0), axis=1, keepdims=True)
        b_col = jnp.sum(jnp.where(sel, gb, 0.0), axis=1, keepdims=True)
        a_row = arow_ref[hd]
        a_prev = a_sc[hd, 0:1, 0:1]
        ct = ct_sc[hd]
        n_row = n_sc[hd, 0:1, :]
        mm = jnp.where(causal, a_row, NEG_INF)
        a_t = jnp.maximum(jnp.max(mm, axis=1, keepdims=True), a_prev)
        s = _dot_nt(q, k) * (dk ** -0.5) * jnp.exp(mm - a_t)
        inter = jnp.exp(a_prev - a_t)
        num = _dot(s.astype(BF16), v) + inter * _dot(q, ct.astype(BF16))
        den = (jnp.sum(s, axis=1, keepdims=True)
               + inter * jnp.sum(q.astype(F32) * n_row, axis=1, keepdims=True))
        h = num / jnp.maximum(jnp.abs(den), jnp.exp(-(b_col + a_t)))
        hm = _sigmoid(o_ref[:, hs].astype(F32)) * h
        hm = hm * lax.rsqrt(jnp.mean(hm * hm, axis=1, keepdims=True) + NORM_EPS) * g_ref[:, hs]
        h_ref[:, hs] = hm.astype(h_ref.dtype)

        a_end = jnp.maximum(jnp.max(a_row, axis=1, keepdims=True), a_prev)
        e_col = jnp.exp(a_col - a_end) * (dk ** -0.5)
        decay = jnp.exp(a_prev - a_end)
        ke = k.astype(F32) * e_col
        ct_new = decay * ct + _dot_tn(ke.astype(BF16), v)
        n_new = decay * n_row + jnp.sum(ke, axis=0, keepdims=True)
        ct_sc[hd] = ct_new
        n_sc[hd] = jnp.broadcast_to(n_new, n_sc.shape[1:])
        a_sc[hd] = jnp.broadcast_to(a_end, a_sc.shape[1:])

        @pl.when(c == pl.num_programs(1) - 1)
        def _():
            c_ref[hd] = ct_new
            n_ref[hd] = n_new
            m_ref[hd] = jnp.broadcast_to(b_col[L - 1:L, :] + a_end, m_ref.shape[1:])


def _mlstm(zm, ga, gb, a_rows, c0t, n0, m0, g_norm, batch, n_heads, seq):
    L = CHUNK
    nc = seq // L
    n = batch * seq
    dh = HEAD_DIM
    hw = n_heads * dh

    def col(off):
        return pl.BlockSpec((L, hw), lambda b, c: (b * nc + c, off))

    def rows():
        return pl.BlockSpec((L, LANES), lambda b, c: (b * nc + c, 0))

    def per_b(r, w):
        return pl.BlockSpec((None, n_heads, r, w), lambda b, c: (b, 0, 0, 0))

    return pl.pallas_call(
        functools.partial(_mlstm_kernel, n_heads=n_heads),
        grid=(batch, nc),
        in_specs=[col(0), col(1), col(2), col(3), rows(), rows(),
                  pl.BlockSpec((None, n_heads, 1, L), lambda b, c: (b, 0, 0, c)),
                  per_b(dh, dh), per_b(1, dh), per_b(1, LANES),
                  pl.BlockSpec((1, hw), lambda b, c: (0, 0))],
        out_specs=[pl.BlockSpec((L, hw), lambda b, c: (b * nc + c, 0)),
                   per_b(dh, dh), per_b(1, dh), per_b(1, LANES)],
        out_shape=[jax.ShapeDtypeStruct((n, hw), BF16),
                   jax.ShapeDtypeStruct((batch, n_heads, dh, dh), F32),
                   jax.ShapeDtypeStruct((batch, n_heads, 1, dh), F32),
                   jax.ShapeDtypeStruct((batch, n_heads, 1, LANES), F32)],
        scratch_shapes=[pltpu.VMEM((n_heads, dh, dh), F32), pltpu.VMEM((n_heads, SUBLANES, dh), F32),
                        pltpu.VMEM((n_heads, SUBLANES, LANES), F32)],
        compiler_params=_params("parallel", "arbitrary"),
        name="mlstm",
    )(zm, zm, zm, zm, ga, gb, a_rows, c0t, n0, m0, g_norm)


def _gla_levels(L):
    levels = []
    w = L // 2
    while w >= 1:
        levels.append(w)
        w //= 2
    return levels


def _gla_consts(L):
    t = np.arange(L)
    row, colj = t[:, None], t[None, :]
    mats = [(colj <= row), (colj > row)]
    masks = []
    for w in _gla_levels(L):
        mid = (t // (2 * w)) * 2 * w + w
        right = t >= mid
        mr = right[:, None] & (colj >= mid[:, None]) & (colj <= row)
        ml = (~right)[:, None] & (colj > row) & (colj < mid[:, None])
        mats.append(mr | ml)
        same = (t[:, None] // (2 * w)) == (t[None, :] // (2 * w))
        masks.append(same & right[:, None] & (~right)[None, :])
    masks.append(row == colj)
    m_all = np.concatenate([m.astype(np.float32) for m in mats], axis=0)
    return m_all, np.stack([m.astype(np.float32) for m in masks])


def _gla_kernel(q_ref, k_ref, v_ref, r_ref, ga_ref, wa_ref, ba_ref, mall_ref, mask_ref, s0_ref, g_ref,
                h_ref, s_ref, st_sc, *, n_valid):
    c = pl.program_id(1)
    L = q_ref.shape[0]
    dk = LANES // 2
    dv = HEAD_DIM
    n_lev = mask_ref.shape[0] - 1
    n_pair = st_sc.shape[0]

    @pl.when(c == 0)
    def _():
        st_sc[...] = s0_ref[...]

    ga = ga_ref[...].astype(BF16)
    mall = mall_ref[...]
    lane = lax.broadcasted_iota(jnp.int32, (1, LANES), 1)
    lm = [(lane < dk).astype(F32), (lane >= dk).astype(F32)]
    for p in range(n_pair):
        ps = slice(p * LANES, (p + 1) * LANES)
        q2 = q_ref[:, ps].astype(F32) * (dk ** -0.5)
        k2 = k_ref[:, ps].astype(F32)
        la = _log_sigmoid(_dot(ga, wa_ref[p]) + ba_ref[p]) * (1.0 / GLA_TAU)
        if n_valid < L:
            valid = lax.broadcasted_iota(jnp.int32, (L, LANES), 0) < n_valid
            la = jnp.where(valid, la, 0.0)
            k2 = jnp.where(valid, k2, 0.0)
        la_hi = la.astype(BF16)
        la_mid = (la - la_hi.astype(F32)).astype(BF16)
        e = jnp.exp(_dot(mall, la_hi) + _dot(mall, la_mid))

        q_in = q2 * e[0:L]
        k_end = (k2 * e[L:2 * L]).astype(BF16)
        decay = e[L - 1:L]
        kq = [(q2 * e[(2 + i) * L:(3 + i) * L], (k2 * e[(2 + i) * L:(3 + i) * L]).astype(BF16))
              for i in range(n_lev)]
        k2b = k2.astype(BF16)
        for hh in range(2):
            hs = slice((2 * p + hh) * dv, (2 * p + hh + 1) * dv)
            v = v_ref[:, hs]
            st = st_sc[p, hh]
            o = _dot_nt((q_in * lm[hh]).astype(BF16), st.astype(BF16))
            a = mask_ref[n_lev] * _dot_nt((q2 * lm[hh]).astype(BF16), k2b)
            for i in range(n_lev):
                qh, kh = kq[i]
                a = a + mask_ref[i] * _dot_nt((qh * lm[hh]).astype(BF16), kh)
            o = o + _dot(a.astype(BF16), v)
            st_sc[p, hh] = decay * st + _dot_tn(v, k_end)
            o = o * lax.rsqrt(jnp.mean(o * o, axis=1, keepdims=True) + NORM_EPS) * g_ref[:, hs]
            r = r_ref[:, hs].astype(F32)
            h_ref[:, hs] = (r * _sigmoid(r) * o).astype(h_ref.dtype)

    @pl.when(c == pl.num_programs(1) - 1)
    def _():
        s_ref[...] = st_sc[...]


def _gla(zg, ga, wa, ba, s0t, g_norm, batch, n_heads, seq, n_valid):
    L = CHUNK
    nc = seq // L
    n = batch * seq
    npair = n_heads // 2
    dv = HEAD_DIM
    m_all, masks = _gla_consts(L)
    m_all = jnp.asarray(m_all, BF16)
    masks = jnp.asarray(masks, F32)
    qkw = npair * LANES
    state = pl.BlockSpec((None, npair, 2, dv, LANES), lambda b, c: (b, 0, 0, 0, 0))

    return pl.pallas_call(
        functools.partial(_gla_kernel, n_valid=n_valid),
        grid=(batch, nc),
        in_specs=[pl.BlockSpec((L, qkw), lambda b, c: (b * nc + c, 0)),
                  pl.BlockSpec((L, qkw), lambda b, c: (b * nc + c, 1)),
                  pl.BlockSpec((L, 2 * qkw), lambda b, c: (b * nc + c, 1)),
                  pl.BlockSpec((L, 2 * qkw), lambda b, c: (b * nc + c, 2)),
                  pl.BlockSpec((L, LANES), lambda b, c: (b * nc + c, 0)),
                  pl.BlockSpec(wa.shape, lambda b, c: (0, 0, 0)),
                  pl.BlockSpec(ba.shape, lambda b, c: (0, 0, 0)),
                  pl.BlockSpec(m_all.shape, lambda b, c: (0, 0)),
                  pl.BlockSpec(masks.shape, lambda b, c: (0, 0, 0)),
                  state,
                  pl.BlockSpec((1, n_heads * dv), lambda b, c: (0, 0))],
        out_specs=[pl.BlockSpec((L, n_heads * dv), lambda b, c: (b * nc + c, 0)), state],
        out_shape=[jax.ShapeDtypeStruct((n, n_heads * dv), BF16),
                   jax.ShapeDtypeStruct((batch, npair, 2, dv, LANES), F32)],
        scratch_shapes=[pltpu.VMEM((npair, 2, dv, LANES), F32)],
        compiler_params=_params("parallel", "arbitrary"),
        name="gla",
    )(zg, zg, zg, zg, ga, wa, ba, m_all, masks, s0t, g_norm)


def _fox_prompt_kernel(qi_ref, ki_ref, q_ref, k_ref, v_ref, gb_ref, crow_ref, o_ref, m_sc, acc_sc, ccol_sc,
                       *, n_heads, lane0):
    g = pl.program_id(0)
    p = pl.program_id(1)
    qi = qi_ref[p]
    ki = ki_ref[p]
    tq = q_ref.shape[0]
    tk = k_ref.shape[0]
    dh = HEAD_DIM
    hpg = q_ref.shape[1] // dh
    groups_per_batch = n_heads // hpg

    @pl.when(ki == 0)
    def _():
        m_sc[...] = jnp.full_like(m_sc, NEG_INF)
        acc_sc[...] = jnp.zeros_like(acc_sc)
        lane = lax.broadcasted_iota(jnp.int32, (tq, LANES), 1)
        gb = gb_ref[...]
        for hh in range(hpg):
            hd = (g % groups_per_batch) * hpg + hh
            ccol_sc[hh] = LOG2E * jnp.sum(jnp.where(lane == lane0 + hd, gb, 0.0), axis=1, keepdims=True)

    def step(diagonal):
        for hh in range(hpg):
            hs = slice(hh * dh, (hh + 1) * dh)
            u = _dot_nt(q_ref[:, hs], k_ref[:, hs]) - LOG2E * crow_ref[hh]
            if diagonal:
                ti = lax.broadcasted_iota(jnp.int32, (tq, tk), 0)
                si = lax.broadcasted_iota(jnp.int32, (tq, tk), 1)
                u = jnp.where(si <= ti, u, NEG_INF)
            c2 = ccol_sc[hh]
            m_old = m_sc[hh]
            m_new = jnp.maximum(m_old, jnp.max(u, axis=1, keepdims=True) + c2)
            pr = jnp.exp2(u - (m_new - c2))
            v1 = jnp.concatenate([v_ref[:, hs], jnp.ones((tk, LANES), BF16)], axis=1)
            acc_sc[hh] = jnp.exp2(m_old - m_new) * acc_sc[hh] + _dot(pr.astype(BF16), v1)
            m_sc[hh] = m_new

    @pl.when(ki < qi)
    def _():
        step(False)

    @pl.when(ki == qi)
    def _():
        step(True)
        for hh in range(hpg):
            acc = acc_sc[hh]
            o_ref[:, hh * dh:(hh + 1) * dh] = (acc[:, :dh] / acc[:, dh:]).astype(o_ref.dtype)


def _fox_prompt(zq, zk, zv, gb, c_rows, batch, n_heads, seq, tile, lane0):
    n = batch * seq
    nq = seq // tile
    dh = HEAD_DIM
    hpg = ATTN_HEADS_PER_STEP
    gpb = n_heads // hpg
    pairs = [(a, b) for a in range(nq) for b in range(a + 1)]
    qi = jnp.asarray([a for a, _ in pairs], jnp.int32)
    ki = jnp.asarray([b for _, b in pairs], jnp.int32)
    c_rows = c_rows.reshape(batch * gpb, hpg, 1, seq)

    def qrow(g, p, qi, ki):
        return (g // gpb) * nq + qi[p]

    def krow(g, p, qi, ki):
        return (g // gpb) * nq + ki[p]

    grid_spec = pltpu.PrefetchScalarGridSpec(
        num_scalar_prefetch=2,
        grid=(batch * gpb, len(pairs)),
        in_specs=[pl.BlockSpec((tile, hpg * dh), lambda g, p, qi, ki: (qrow(g, p, qi, ki), g % gpb)),
                  pl.BlockSpec((tile, hpg * dh), lambda g, p, qi, ki: (krow(g, p, qi, ki), g % gpb)),
                  pl.BlockSpec((tile, hpg * dh), lambda g, p, qi, ki: (krow(g, p, qi, ki), g % gpb)),
                  pl.BlockSpec((tile, LANES), lambda g, p, qi, ki: (qrow(g, p, qi, ki), 0)),
                  pl.BlockSpec((None, hpg, 1, tile), lambda g, p, qi, ki: (g, 0, 0, ki[p]))],
        out_specs=pl.BlockSpec((tile, hpg * dh), lambda g, p, qi, ki: (qrow(g, p, qi, ki), g % gpb)),
        scratch_shapes=[pltpu.VMEM((hpg, tile, 1), F32), pltpu.VMEM((hpg, tile, dh + LANES), F32),
                        pltpu.VMEM((hpg, tile, 1), F32)],
    )
    return pl.pallas_call(
        functools.partial(_fox_prompt_kernel, n_heads=n_heads, lane0=lane0),
        grid_spec=grid_spec,
        out_shape=jax.ShapeDtypeStruct((n, n_heads * dh), BF16),
        compiler_params=_params("parallel", "arbitrary"),
        name="fox_prompt",
    )(qi, ki, zq, zk, zv, gb, c_rows)


def _fox_sample_kernel(pt_ref, q_ref, cq_ref, kn_ref, vn_ref, bn_ref, *rest, n_heads, n_new, group):
    k_refs, v_refs, lf_refs = rest[:group], rest[group:2 * group], rest[2 * group:3 * group]
    o_ref, m_sc, l_sc, acc_sc, carry_sc = rest[3 * group:]
    j = pl.program_id(1)
    nr = q_ref.shape[0]
    rows_pp = k_refs[0].shape[0]
    d = q_ref.shape[1]
    n_tiles = rows_pp // LANES
    scale = d ** -0.5

    @pl.when(j == 0)
    def _():
        m_sc[...] = jnp.full_like(m_sc, NEG_INF)
        l_sc[...] = jnp.zeros_like(l_sc)
        acc_sc[...] = jnp.zeros_like(acc_sc)
        carry_sc[...] = jnp.zeros_like(carry_sc)

    lane8 = lax.broadcasted_iota(jnp.int32, (SUBLANES, LANES), 1)
    row8 = lax.broadcasted_iota(jnp.int32, (SUBLANES, LANES), 0)

    def page_suffix(x):
        y = x
        z = x
        sh = n_heads
        while sh < LANES:
            y = y + jnp.where(lane8 + sh < LANES, pltpu.roll(y, LANES - sh, 1), 0.0)
            z = z + pltpu.roll(z, sh, 1)
            sh *= 2
        w = z
        sh = 1
        while sh < SUBLANES:
            w = w + jnp.where(row8 + sh < SUBLANES, pltpu.roll(w, SUBLANES - sh, 0), 0.0)
            sh *= 2
        return y - x + (w - z), jnp.broadcast_to(w[0:1, :], x.shape)

    q = q_ref[...]
    cq = cq_ref[...]
    rowi = lax.broadcasted_iota(jnp.int32, (nr, LANES), 0)
    lanei = lax.broadcasted_iota(jnp.int32, (nr, LANES), 1)
    head_ok = (rowi // n_new) == (lanei % n_heads)

    def update(tile_groups, vbs):
        flat = [t for ts in tile_groups for t in ts]
        m_old = m_sc[...]
        mx = flat[0]
        for t in flat[1:]:
            mx = jnp.maximum(mx, t)
        m_new = jnp.maximum(m_old, jnp.max(mx, axis=1, keepdims=True))
        alpha = jnp.exp(m_old - m_new)
        tot = None
        acc = alpha * acc_sc[...]
        for ts, vb in zip(tile_groups, vbs):
            ps = [jnp.exp(t - m_new) for t in ts]
            for t in ps:
                tot = t if tot is None else tot + t
            pcat = ps[0] if len(ps) == 1 else jnp.concatenate(ps, axis=1)
            acc = acc + _dot(pcat.astype(BF16), vb)
        l_sc[...] = alpha * l_sc[...] + jnp.sum(tot, axis=1, keepdims=True)
        acc_sc[...] = acc
        m_sc[...] = m_new

    carry = carry_sc[...]
    tile_groups = []
    for g in range(group):
        within, total = page_suffix(lf_refs[g][...])
        suffix = within + carry
        carry = carry + total
        s = _dot_nt(q, k_refs[g][...].astype(BF16)) * scale
        tile_groups.append([jnp.where(head_ok, s[:, r * LANES:(r + 1) * LANES] + cq + suffix[r:r + 1, :], NEG_INF)
                            for r in range(n_tiles)])
    carry_sc[...] = carry
    update(tile_groups, [v_refs[g][...].astype(BF16) for g in range(group)])

    @pl.when(j == pl.num_programs(1) - 1)
    def _():
        s_self = _dot_nt(q, kn_ref[...]) * scale + cq + bn_ref[...]
        ok = head_ok & (lanei < n_new * n_heads) & ((lanei // n_heads) <= (rowi % n_new))
        update([[jnp.where(ok, s_self, NEG_INF)]], [vn_ref[...]])
        o_ref[...] = acc_sc[...] / l_sc[...]


def _fox_sample(page_table, q, cq, kn, vn, bn, cache_k, cache_v, cache_lf, layer, n_heads, n_new):
    nb, n_pages = page_table.shape
    nr = q.shape[1]
    rows_pp = cache_k.shape[2]
    d = q.shape[2]
    group = max(g for g in (1, 2, 4, 8) if n_pages % g == 0)

    def page_spec(rows, width, g):
        return pl.BlockSpec((None, None, rows, width),
                            lambda b, j, pt: (layer, pt[b, n_pages - 1 - (j * group + g)], 0, 0))

    per_b = lambda rows, width: pl.BlockSpec((None, rows, width), lambda b, j, pt: (b, 0, 0))
    grid_spec = pltpu.PrefetchScalarGridSpec(
        num_scalar_prefetch=1,
        grid=(nb, n_pages // group),
        in_specs=([per_b(nr, d), per_b(nr, LANES), per_b(LANES, d), per_b(LANES, d), per_b(1, LANES)]
                  + [page_spec(rows_pp, d, g) for g in range(group)]
                  + [page_spec(rows_pp, d, g) for g in range(group)]
                  + [page_spec(SUBLANES, LANES, g) for g in range(group)]),
        out_specs=per_b(nr, d),
        scratch_shapes=[pltpu.VMEM((nr, 1), F32), pltpu.VMEM((nr, 1), F32), pltpu.VMEM((nr, d), F32),
                        pltpu.VMEM((SUBLANES, LANES), F32)],
    )
    return pl.pallas_call(
        functools.partial(_fox_sample_kernel, n_heads=n_heads, n_new=n_new, group=group),
        grid_spec=grid_spec,
        out_shape=jax.ShapeDtypeStruct((nb, nr, d), F32),
        compiler_params=_params("parallel", "arbitrary"),
        name="fox_sample",
    )(page_table, q, cq, kn, vn, bn, *([cache_k] * group), *([cache_v] * group), *([cache_lf] * group))


def _layer_norm(r, g, b):
    mu = jnp.mean(r, axis=1, keepdims=True)
    xc = r - mu
    var = jnp.mean(xc * xc, axis=1, keepdims=True)
    return xc * lax.rsqrt(var + LN_EPS) * g + b


def _outproj_kernel(hm_ref, hg_ref, hf_ref, wm_ref, wg_ref, wf_ref, x_ref, g_ref, b_ref, h_ref, hb_ref, *, alpha):
    mix = _dot(hm_ref[...], wm_ref[...]) + _dot(hg_ref[...], wg_ref[...]) + _dot(hf_ref[...], wf_ref[...])
    h = _layer_norm(alpha * x_ref[...] + mix, g_ref[...], b_ref[...])
    h_ref[...] = h
    hb_ref[...] = h.astype(BF16)


def _outproj_ln(hm, hg, hf, w_out, x, g, b, alpha, tm):
    n, d = x.shape
    tm = min(tm, n)
    wm, wg, wf = hm.shape[1], hg.shape[1], hf.shape[1]
    assert wm == wg and wf == wm + wg and n % tm == 0
    const = pl.Buffered(1)
    return pl.pallas_call(
        functools.partial(_outproj_kernel, alpha=alpha),
        grid=(n // tm,),
        in_specs=[pl.BlockSpec((tm, wm), lambda i: (i, 0)),
                  pl.BlockSpec((tm, wg), lambda i: (i, 0)),
                  pl.BlockSpec((tm, wf), lambda i: (i, 0)),
                  pl.BlockSpec((wm, d), lambda i: (0, 0), pipeline_mode=const),
                  pl.BlockSpec((wg, d), lambda i: (1, 0), pipeline_mode=const),
                  pl.BlockSpec((wf, d), lambda i: (1, 0), pipeline_mode=const),
                  pl.BlockSpec((tm, d), lambda i: (i, 0)),
                  pl.BlockSpec((1, d), lambda i: (0, 0)),
                  pl.BlockSpec((1, d), lambda i: (0, 0))],
        out_specs=[pl.BlockSpec((tm, d), lambda i: (i, 0))] * 2,
        out_shape=[jax.ShapeDtypeStruct((n, d), F32), jax.ShapeDtypeStruct((n, d), BF16)],
        compiler_params=_params("parallel"),
        name="outproj_ln",
    )(hm, hg, hf, w_out, w_out, w_out, x, g, b)


def _ffn_up_kernel(x_ref, wg_ref, wv_ref, cwg_ref, cwv_ref, cbg_ref, cbv_ref, hg0_ref, hv0_ref,
                   o_ref, tg_ref, tv_ref, halo_g, halo_v, win_g, win_v, *, tiles_per_seq, shift):
    i = pl.program_id(0)
    j = pl.program_id(1)
    tm = x_ref.shape[0]
    hr = hg0_ref.shape[0]
    tn = o_ref.shape[1]

    @pl.when(i % tiles_per_seq == 0)
    def _():
        halo_g[j] = hg0_ref[...]
        halo_v[j] = hv0_ref[...]

    rc = min(FFN_ROW_CHUNK, tm)

    rb = min(FFN_EPILOGUE_ROWS, rc)

    def conv(r0, cs, cw_ref, cb_ref, win):
        cw = cw_ref[:, cs]
        return (cb_ref[:, cs] + cw[0:1] * win[pl.ds(hr + r0 - 2 * shift, rb), cs]
                + cw[1:2] * win[pl.ds(hr + r0 - shift, rb), cs] + cw[2:3] * win[pl.ds(hr + r0, rb), cs])

    win_g[0:hr, :] = halo_g[j]
    win_v[0:hr, :] = halo_v[j]
    for c0 in range(0, tn, FFN_COL_SLAB):
        cs = slice(c0, c0 + FFN_COL_SLAB)
        for r0 in range(0, tm, rc):
            rs = slice(r0, r0 + rc)
            ug = _dot(x_ref[rs, :], wg_ref[:, cs])
            uv = _dot(x_ref[rs, :], wv_ref[:, cs])
            win_g[hr + r0:hr + r0 + rc, cs] = ug
            win_v[hr + r0:hr + r0 + rc, cs] = uv
            for r1 in range(r0, r0 + rc, rb):
                yg = conv(r1, cs, cwg_ref, cbg_ref, win_g)
                yv = conv(r1, cs, cwv_ref, cbv_ref, win_v)
                o_ref[r1:r1 + rb, cs] = (yg * _sigmoid(yg) * yv).astype(o_ref.dtype)
    tail_g = win_g[tm:tm + hr, :]
    tail_v = win_v[tm:tm + hr, :]
    tg_ref[...] = tail_g
    tv_ref[...] = tail_v
    halo_g[j] = tail_g
    halo_v[j] = tail_v


def _ffn_up(x, w_up, conv_w, conv_b, halo0, fp, tm, tn, tiles_per_seq, shift):
    n, d = x.shape
    tm = min(tm, n)
    nj = fp // tn
    ni = n // tm
    hr = halo0.shape[0] // (ni // tiles_per_seq)
    assert n % tm == 0 and fp % tn == 0 and hr >= 2 * shift
    seq = lambda i: i // tiles_per_seq
    return pl.pallas_call(
        functools.partial(_ffn_up_kernel, tiles_per_seq=tiles_per_seq, shift=shift),
        grid=(ni, nj),
        in_specs=[pl.BlockSpec((tm, d), lambda i, j: (i, 0)),
                  pl.BlockSpec((d, tn), lambda i, j: (0, j)),
                  pl.BlockSpec((d, tn), lambda i, j: (0, nj + j)),
                  pl.BlockSpec((CONV_W, tn), lambda i, j: (0, j)),
                  pl.BlockSpec((CONV_W, tn), lambda i, j: (0, nj + j)),
                  pl.BlockSpec((1, tn), lambda i, j: (0, j)),
                  pl.BlockSpec((1, tn), lambda i, j: (0, nj + j)),
                  pl.BlockSpec((hr, tn), lambda i, j: (seq(i), j)),
                  pl.BlockSpec((hr, tn), lambda i, j: (seq(i), nj + j))],
        out_specs=[pl.BlockSpec((tm, tn), lambda i, j: (i, j)),
                   pl.BlockSpec((hr, tn), lambda i, j: (i, j)),
                   pl.BlockSpec((hr, tn), lambda i, j: (i, j))],
        out_shape=[jax.ShapeDtypeStruct((n, fp), BF16),
                   jax.ShapeDtypeStruct((ni * hr, fp), F32),
                   jax.ShapeDtypeStruct((ni * hr, fp), F32)],
        scratch_shapes=[pltpu.VMEM((nj, hr, tn), F32), pltpu.VMEM((nj, hr, tn), F32),
                        pltpu.VMEM((hr + tm, tn), F32), pltpu.VMEM((hr + tm, tn), F32)],
        compiler_params=_params("arbitrary", "arbitrary"),
        name="ffn_up",
    )(x, w_up, w_up, conv_w, conv_w, conv_b, conv_b, halo0, halo0)


def _ffn_down_kernel(a_ref, w_ref, h_ref, g_ref, b_ref, x_ref, xb_ref, *, alpha):
    x = _layer_norm(alpha * h_ref[...] + _dot(a_ref[...], w_ref[...]), g_ref[...], b_ref[...])
    x_ref[...] = x
    xb_ref[...] = x.astype(BF16)


def _ffn_down_ln(a, w_down, h, g, b, alpha, tm):
    n, fp = a.shape
    d = w_down.shape[1]
    tm = min(tm, n)
    assert n % tm == 0
    return pl.pallas_call(
        functools.partial(_ffn_down_kernel, alpha=alpha),
        grid=(n // tm,),
        in_specs=[pl.BlockSpec((tm, fp), lambda i: (i, 0)),
                  pl.BlockSpec((fp, d), lambda i: (0, 0), pipeline_mode=pl.Buffered(1)),
                  pl.BlockSpec((tm, d), lambda i: (i, 0)),
                  pl.BlockSpec((1, d), lambda i: (0, 0)),
                  pl.BlockSpec((1, d), lambda i: (0, 0))],
        out_specs=[pl.BlockSpec((tm, d), lambda i: (i, 0))] * 2,
        out_shape=[jax.ShapeDtypeStruct((n, d), F32), jax.ShapeDtypeStruct((n, d), BF16)],
        compiler_params=_params("parallel"),
        name="ffn_down_ln",
    )(a, w_down, h, g, b)


def _prep_layer(l, dims, w_in, b_m_ig, b_m_fg, w_g_alpha_up, b_g_alpha, b_f, g_m_norm, g_g_norm, w_out,
                ln1_g, ln1_b, w_up, conv_w, conv_b, w_down, ln2_g, ln2_b):
    d, hm, hg, hf, f, fp = dims
    dh = HEAD_DIM
    dkg = dh // 2
    sizes = [hm * dh] * 4 + [hm, hm] + [hg * dkg] * 2 + [hg * dh] * 2 + [GLA_RANK] + [hf * dh] * 3 + [hf]
    offs = np.concatenate([[0], np.cumsum(sizes)]).tolist()
    (o_mq, o_mk, o_mv, o_mo, o_mi, o_mf, o_gq, o_gk, o_gv, o_gr, o_ga, o_fq, o_fk, o_fv, o_ff, _) = offs
    w = w_in[l]
    col = lambda a, b: w[:, a:b]
    zeros = lambda c: jnp.zeros((d, c), F32)
    blk_a = jnp.concatenate([col(o_mi, o_mi + hm), zeros(8 - hm), col(o_ga, o_ga + GLA_RANK),
                             zeros(LANES - 8 - GLA_RANK)], axis=1)
    blk_b = jnp.concatenate([col(o_mf, o_mf + hm), zeros(8 - hm), col(o_ff, o_ff + hf),
                             zeros(LANES - 8 - hf)], axis=1)
    pad1 = lambda v, lo, total: jnp.pad(v, (lo, total - lo - v.shape[0]))[None, :]
    wa = jnp.zeros((LANES, hg * dkg), F32).at[8:8 + GLA_RANK].set(w_g_alpha_up[l])
    npair = hg // 2
    pad_f = lambda a: jnp.pad(a, [(0, 0)] * (a.ndim - 1) + [(0, fp - f)])
    return dict(
        w_m=col(o_mq, o_mi).astype(BF16),
        w_g=col(o_gq, o_ga).astype(BF16),
        w_fq=col(o_fq, o_fk).astype(BF16),
        w_fk=col(o_fk, o_fv).astype(BF16),
        w_fv=col(o_fv, o_ff).astype(BF16),
        w_small=jnp.concatenate([blk_a, blk_b], axis=1).astype(BF16),
        bias_a=pad1(b_m_ig[l], 0, LANES),
        bias_b=pad1(b_m_fg[l], 0, LANES) + pad1(b_f[l], 8, LANES),
        wa=wa.reshape(LANES, npair, LANES).transpose(1, 0, 2).astype(BF16),
        ba=b_g_alpha[l].reshape(npair, 1, LANES),
        g_m=g_m_norm[l][None, :], g_g=g_g_norm[l][None, :],
        w_out=w_out[l].astype(BF16),
        ln1_g=ln1_g[l][None, :], ln1_b=ln1_b[l][None, :], ln2_g=ln2_g[l][None, :], ln2_b=ln2_b[l][None, :],
        w_up=_cast_pad(w_up, l, 2, 2, f, fp),
        conv_w=jnp.concatenate([pad_f(conv_w[l][:, :f]), pad_f(conv_w[l][:, f:])], axis=1),
        conv_b=jnp.concatenate([pad_f(conv_b[l][:f]), pad_f(conv_b[l][f:])])[None, :],
        w_down=_cast_pad(w_down, l, 1, 1, f, fp),
    )


def _pad_conv_state(s, f, fp):
    pad = [(0, 0)] * (s.ndim - 1) + [(0, fp - f)]
    return jnp.concatenate([jnp.pad(s[..., :f], pad), jnp.pad(s[..., f:], pad)], axis=-1)


def _mixer_rows(x_bf, p, tm, q_scale=1.0):
    zm, = _matmul(x_bf, p["w_m"], [BF16], tm, 512, "proj_m")
    zg, = _matmul(x_bf, p["w_g"], [BF16], tm, 512, "proj_g")
    zq, = _matmul(x_bf, p["w_fq"], [BF16], tm, 512, "proj_fq", scale=q_scale)
    zk, zkb = _matmul(x_bf, p["w_fk"], [F32, BF16], tm, 512, "proj_fk")
    zv, zvb = _matmul(x_bf, p["w_fv"], [F32, BF16], tm, 512, "proj_fv")
    return zm, zg, zq, zk, zkb, zv, zvb


def kernel(x_prompt, x_sample, state_mlstm_c, state_mlstm_n, state_mlstm_m, state_gla, state_ffn_conv,
           cache_k, cache_v, cache_logf, page_table,
           w_in, b_m_ig, b_m_fg, w_g_alpha_up, b_g_alpha, b_f, g_m_norm, g_g_norm, w_out,
           ln1_g, ln1_b, w_up, conv_w, conv_b, w_down, ln2_g, ln2_b):
    bp, seq, d = x_prompt.shape
    db, ns, _ = x_sample.shape
    depth = w_in.shape[0]
    hm = b_m_ig.shape[1]
    hf = b_f.shape[1]
    dh = HEAD_DIM
    hg = g_g_norm.shape[1] // dh
    dkg = dh // 2
    f = w_down.shape[1]
    fp = -(-f // 512) * 512
    alpha = (2.0 * depth) ** 0.25
    n_pool, page = cache_k.shape[1], cache_k.shape[2]
    assert hm <= 8 and hf <= 8 and hg % 2 == 0 and seq % 512 == 0 and ns <= CHUNK
    assert page * hf == SUBLANES * LANES and ns * hf <= LANES
    dims = (d, hm, hg, hf, f, fp)
    L = CHUNK
    n_p = bp * seq

    tri_p = jnp.asarray(np.tril(np.ones((GATE_TILE, GATE_TILE), np.float32)), BF16)
    gs_rows = LANES
    r = np.arange(gs_rows)
    tri_s_np = ((r[:, None] % db == r[None, :] % db) & (r[None, :] // db <= r[:, None] // db)
                & (r[:, None] < ns * db) & (r[None, :] < ns * db))
    tri_s = jnp.asarray(tri_s_np.astype(np.float32), BF16)

    xp = x_prompt.astype(F32).reshape(n_p, d)
    xp_bf = xp.astype(BF16)
    xs = x_sample.astype(F32).transpose(1, 0, 2).reshape(ns * db, d)
    xs_bf = xs.astype(BF16)

    ck = cache_k.astype(F32).reshape(depth, n_pool, page * hf, dh)
    cv = cache_v.astype(F32).reshape(depth, n_pool, page * hf, dh)
    clf = cache_logf.astype(F32).reshape(depth, n_pool, SUBLANES, LANES)

    def to_padded(z, pad_value=0.0, mode="constant"):
        c = z.shape[-1]
        a = z.reshape(ns, db, c).transpose(1, 0, 2)
        if mode == "edge":
            a = jnp.pad(a, ((0, 0), (0, L - ns), (0, 0)), mode="edge")
        else:
            a = jnp.pad(a, ((0, 0), (0, L - ns), (0, 0)), constant_values=pad_value)
        return a.reshape(db * L, c)

    def from_padded(y):
        c = y.shape[-1]
        return y.reshape(db, L, c)[:, :ns].transpose(1, 0, 2).reshape(ns * db, c)

    def head_rows(g, lane0, nh, batch, t):
        return g[:, lane0:lane0 + nh].reshape(batch, t, nh).transpose(0, 2, 1).reshape(batch * nh, 1, t)

    outs_p = [[] for _ in range(8)]
    outs_s = [[] for _ in range(8)]
    for l in range(depth):
        p = _prep_layer(l, dims, w_in, b_m_ig, b_m_fg, w_g_alpha_up, b_g_alpha, b_f, g_m_norm, g_g_norm, w_out,
                        ln1_g, ln1_b, w_up, conv_w, conv_b, w_down, ln2_g, ln2_b)

        zm, zg, zq, zk, zkb, zv, zvb = _mixer_rows(xp_bf, p, 1024, q_scale=LOG2E * dh ** -0.5)
        ga, gb, gc = _gates(xp_bf, p["w_small"], p["bias_a"], p["bias_b"], tri_p, seq // GATE_TILE, hm)
        zeros = lambda *s: jnp.zeros(s, F32)
        h_m, ct, n_m, m_m = _mlstm(zm, ga, gb, head_rows(ga, 0, hm, bp, seq).reshape(bp, hm, 1, seq),
                                   zeros(bp, hm, dh, dh), zeros(bp, hm, 1, dh), zeros(bp, hm, 1, LANES),
                                   p["g_m"], bp, hm, seq)
        h_g, st = _gla(zg, ga, p["wa"], p["ba"], zeros(bp, hg // 2, 2, dh, LANES), p["g_g"], bp, hg, seq, L)
        h_f = _fox_prompt(zq, zkb, zvb, gb, head_rows(gb, 8, hf, bp, seq), bp, hf, seq, ATTN_TILE, 8)
        h, h_bf = _outproj_ln(h_m, h_g, h_f, p["w_out"], xp, p["ln1_g"], p["ln1_b"], alpha, 512)
        tm_f = min(FFN_ROW_TILE, seq)
        act, tg, tv = _ffn_up(h_bf, p["w_up"], p["conv_w"], p["conv_b"], zeros(bp * SUBLANES, 2 * fp),
                              fp, tm_f, 512, seq // tm_f, 1)
        xp, xp_bf = _ffn_down_ln(act, p["w_down"], h, p["ln2_g"], p["ln2_b"], alpha, FFN_DOWN_ROW_TILE)

        def conv_tail_p(t):
            t = t.reshape(bp, seq // tm_f, SUBLANES, fp)[:, -1, SUBLANES - (CONV_W - 1):, :f]
            return t
        outs_p[0].append(ct.reshape(bp, hm, dh, dh).swapaxes(-1, -2))
        outs_p[1].append(n_m.reshape(bp, hm, dh))
        outs_p[2].append(m_m[:, :, 0, 0])
        st_h = st.reshape(bp, hg // 2, 2, dh, 2, dkg)
        st_h = jnp.stack([st_h[:, :, 0, :, 0, :], st_h[:, :, 1, :, 1, :]], axis=2)
        outs_p[3].append(st_h.reshape(bp, hg, dh, dkg).swapaxes(-1, -2))
        outs_p[4].append(jnp.concatenate([conv_tail_p(tg), conv_tail_p(tv)], axis=-1))
        outs_p[5].append(zk.reshape(bp, seq, hf, dh))
        outs_p[6].append(zv.reshape(bp, seq, hf, dh))
        outs_p[7].append(gc[:, 8:8 + hf].reshape(bp, seq, hf))

        n_s = ns * db
        zm, zg, zq, zk, zkb, zv, zvb = _mixer_rows(xs_bf, p, n_s)
        xs_pad = jnp.pad(xs_bf, ((0, gs_rows - n_s), (0, 0)))
        ga, gb, gc = _gates(xs_pad, p["w_small"], p["bias_a"], p["bias_b"], tri_s, 1, hm)
        ga, gb, gc = ga[:n_s], gb[:n_s], gc[:n_s]
        ga_p = to_padded(ga, NEG_INF)
        gb_p = to_padded(gb, mode="edge")
        c0t = state_mlstm_c[l].astype(F32).swapaxes(-1, -2)
        n0 = state_mlstm_n[l].astype(F32).reshape(db, hm, 1, dh)
        m0 = jnp.broadcast_to(state_mlstm_m[l].astype(F32).reshape(db, hm, 1, 1), (db, hm, 1, LANES))
        h_m, ct, n_m, m_m = _mlstm(to_padded(zm), ga_p, gb_p, head_rows(ga_p, 0, hm, db, L).reshape(db, hm, 1, L),
                                   c0t, n0, m0, p["g_m"], db, hm, L)
        s0 = state_gla[l].astype(F32).swapaxes(-1, -2).reshape(db, hg // 2, 2, dh, dkg)
        s0t = jnp.stack([jnp.pad(s0[:, :, 0], ((0, 0), (0, 0), (0, 0), (0, dkg))),
                         jnp.pad(s0[:, :, 1], ((0, 0), (0, 0), (0, 0), (dkg, 0)))], axis=2)
        h_g, st = _gla(to_padded(zg), to_padded(ga), p["wa"], p["ba"], s0t, p["g_g"], db, hg, L, ns)

        q_s = zq.reshape(ns, db, hf, dh).transpose(1, 2, 0, 3).reshape(db, hf * ns, dh)
        new_rows = lambda z: jnp.pad(z.reshape(ns, db, hf * dh).transpose(1, 0, 2).reshape(db, ns * hf, dh),
                                     ((0, 0), (0, LANES - ns * hf), (0, 0)))
        c_new = gb[:, 8:8 + hf].reshape(ns, db, hf)
        cq = jnp.broadcast_to(c_new.transpose(1, 2, 0).reshape(db, hf * ns, 1), (db, hf * ns, LANES))
        bn = jnp.pad(-c_new.transpose(1, 0, 2).reshape(db, 1, ns * hf), ((0, 0), (0, 0), (0, LANES - ns * hf)))
        o_f = _fox_sample(page_table, q_s, cq, new_rows(zkb), new_rows(zvb), bn, ck, cv, clf, l, hf, ns)
        h_f = o_f.reshape(db, hf, ns, dh).transpose(2, 0, 1, 3).reshape(n_s, hf * dh).astype(BF16)

        h, h_bf = _outproj_ln(from_padded(h_m), from_padded(h_g), h_f, p["w_out"], xs, p["ln1_g"], p["ln1_b"],
                              alpha, n_s)
        halo_s = _pad_conv_state(state_ffn_conv[l].astype(F32).transpose(1, 0, 2).reshape((CONV_W - 1) * db, 2 * f),
                                 f, fp)
        act, tg, tv = _ffn_up(h_bf, p["w_up"], p["conv_w"], p["conv_b"], halo_s, fp, n_s, 512, 1, db)
        xs, xs_bf = _ffn_down_ln(act, p["w_down"], h, p["ln2_g"], p["ln2_b"], alpha, n_s)

        conv_tail_s = lambda t: t[:, :f].reshape(CONV_W - 1, db, f).transpose(1, 0, 2)
        outs_s[0].append(ct.reshape(db, hm, dh, dh).swapaxes(-1, -2))
        outs_s[1].append(n_m.reshape(db, hm, dh))
        outs_s[2].append(m_m[:, :, 0, 0])
        st_h = st.reshape(db, hg // 2, 2, dh, 2, dkg)
        st_h = jnp.stack([st_h[:, :, 0, :, 0, :], st_h[:, :, 1, :, 1, :]], axis=2)
        outs_s[3].append(st_h.reshape(db, hg, dh, dkg).swapaxes(-1, -2))
        outs_s[4].append(jnp.concatenate([conv_tail_s(tg), conv_tail_s(tv)], axis=-1))
        outs_s[5].append(zk.reshape(ns, db, hf, dh).transpose(1, 0, 2, 3))
        outs_s[6].append(zv.reshape(ns, db, hf, dh).transpose(1, 0, 2, 3))
        outs_s[7].append(gc[:, 8:8 + hf].reshape(ns, db, hf).transpose(1, 0, 2))

    y_p = xp.reshape(bp, seq, d)
    y_s = xs.reshape(ns, db, d).transpose(1, 0, 2)
    return (y_p, y_s) + tuple(jnp.stack(a) for a in outs_p) + tuple(jnp.stack(a) for a in outs_s)
```

```python
import functools

import numpy as np
import jax
import jax.numpy as jnp
from jax import lax
from jax.experimental import pallas as pl
from jax.experimental.pallas import tpu as pltpu

F32 = jnp.float32
BF16 = jnp.bfloat16

HEAD_DIM = 128
GLA_RANK = 16
GLA_TAU = 16.0
CONV_W = 3
LN_EPS = 1e-5
NORM_EPS = 1e-6

LANES = 128
SUBLANES = 8
VMEM_LIMIT = 56 * 1024 * 1024

CHUNK = 128
GATE_TILE = 256
ATTN_TILE = 512
ATTN_HEADS_PER_STEP = 8
FFN_ROW_TILE = 1024
FFN_COL_SLAB = 256
FFN_ROW_CHUNK = 512
FFN_EPILOGUE_ROWS = 512
FFN_DOWN_ROW_TILE = 256
NEG_INF = float("-inf")
LOG2E = 1.4426950408889634


def _params(*sem, flags=None):
    return pltpu.CompilerParams(dimension_semantics=sem, vmem_limit_bytes=VMEM_LIMIT, flags=flags)


def _dot(a, b):
    return jnp.dot(a, b, preferred_element_type=F32)


def _dot_nt(a, b):
    return lax.dot_general(a, b, (((1,), (1,)), ((), ())), preferred_element_type=F32)


def _dot_tn(a, b):
    return lax.dot_general(a, b, (((0,), (0,)), ((), ())), preferred_element_type=F32)


def _log_sigmoid(x):
    return jnp.minimum(x, 0.0) - jnp.log1p(jnp.exp(-jnp.abs(x)))


def _sigmoid(x):
    return 1.0 / (1.0 + jnp.exp(-x))


def _split3(x):
    hi = x.astype(BF16)
    r = x - hi.astype(F32)
    mid = r.astype(BF16)
    lo = (r - mid.astype(F32)).astype(BF16)
    return hi, mid, lo


def _mm_kernel(x_ref, w_ref, *o_refs, scale):
    acc = _dot(x_ref[...], w_ref[...])
    if scale != 1.0:
        acc = acc * scale
    for o in o_refs:
        o[...] = acc.astype(o.dtype)


def _matmul(x, w, out_dtypes, tm, tn, name, scale=1.0):
    m, k = x.shape
    n = w.shape[1]
    tm = min(tm, m)
    tn = min(tn, n)
    assert m % tm == 0 and n % tn == 0
    return pl.pallas_call(
        functools.partial(_mm_kernel, scale=scale),
        grid=(m // tm, n // tn),
        in_specs=[pl.BlockSpec((tm, k), lambda i, j: (i, 0)),
                  pl.BlockSpec((k, tn), lambda i, j: (0, j))],
        out_specs=[pl.BlockSpec((tm, tn), lambda i, j: (i, j)) for _ in out_dtypes],
        out_shape=[jax.ShapeDtypeStruct((m, n), d) for d in out_dtypes],
        compiler_params=_params("parallel", "parallel"),
        name=name,
    )(x, w)


def _cast_pad_rows_kernel(x_ref, o_ref):
    r = x_ref.shape[0]
    o_ref[0:r, :] = x_ref[...].astype(o_ref.dtype)
    if o_ref.shape[0] > r:
        o_ref[r:, :] = jnp.zeros((o_ref.shape[0] - r, o_ref.shape[1]), o_ref.dtype)


def _cast_pad_rows(w, layer, rows_padded, tc):
    _, r, c = w.shape
    assert r % 16 == 0 and rows_padded % 16 == 0 and c % tc == 0
    return pl.pallas_call(
        _cast_pad_rows_kernel,
        grid=(c // tc,),
        in_specs=[pl.BlockSpec((None, r, tc), lambda j: (layer, 0, j))],
        out_specs=pl.BlockSpec((rows_padded, tc), lambda j: (0, j)),
        out_shape=jax.ShapeDtypeStruct((rows_padded, c), BF16),
        compiler_params=_params("parallel"),
        name="cast_pad_rows",
    )(w)


def _cast_pad_cols_kernel(x_ref, o_ref, *, n_parts, part, part_padded):
    for p in range(n_parts):
        o_ref[:, p * part_padded:p * part_padded + part] = x_ref[:, p * part:(p + 1) * part].astype(o_ref.dtype)
        if part_padded > part:
            o_ref[:, p * part_padded + part:(p + 1) * part_padded] = jnp.zeros(
                (o_ref.shape[0], part_padded - part), o_ref.dtype)


def _cast_pad_cols(w, layer, n_parts, part_padded, tr):
    _, r, c = w.shape
    part = c // n_parts
    assert part % LANES == 0 and part_padded % LANES == 0 and r % tr == 0
    return pl.pallas_call(
        functools.partial(_cast_pad_cols_kernel, n_parts=n_parts, part=part, part_padded=part_padded),
        grid=(r // tr,),
        in_specs=[pl.BlockSpec((None, tr, c), lambda i: (layer, i, 0))],
        out_specs=pl.BlockSpec((tr, n_parts * part_padded), lambda i: (i, 0)),
        out_shape=jax.ShapeDtypeStruct((r, n_parts * part_padded), BF16),
        compiler_params=_params("parallel"),
        name="cast_pad_cols",
    )(w)


def _cast_groups_kernel(x_ref, *o_refs, bounds):
    for o_ref, (lo, hi) in zip(o_refs, bounds):
        o_ref[...] = x_ref[:, lo:hi].astype(o_ref.dtype)


def _cast_groups(w, layer, bounds, tr):
    _, r, c = w.shape
    assert r % tr == 0
    return pl.pallas_call(
        functools.partial(_cast_groups_kernel, bounds=tuple(bounds)),
        grid=(r // tr,),
        in_specs=[pl.BlockSpec((None, tr, c), lambda i: (layer, i, 0))],
        out_specs=[pl.BlockSpec((tr, hi - lo), lambda i: (i, 0)) for lo, hi in bounds],
        out_shape=[jax.ShapeDtypeStruct((r, hi - lo), BF16) for lo, hi in bounds],
        compiler_params=_params("parallel"),
        name="cast_groups",
    )(w)


def _gates_kernel(x_ref, w_ref, ba_ref, bb_ref, tri_ref, oa_ref, ob_ref, oc_ref, carry, *, tiles_per_seq, n_ig):
    i = pl.program_id(0)
    tm = x_ref.shape[0]
    z = _dot(x_ref[...], w_ref[...])
    za = z[:, :LANES] + ba_ref[...]
    ls = _log_sigmoid(z[:, LANES:] + bb_ref[...])
    tri = tri_ref[...]
    hi, mid, lo = _split3(ls)
    cum = _dot(tri, hi) + _dot(tri, mid) + _dot(tri, lo)
    if tiles_per_seq > 1:
        @pl.when(i % tiles_per_seq == 0)
        def _():
            carry[...] = jnp.zeros_like(carry)
        cum = cum + carry[0:1, :]
        carry[...] = jnp.broadcast_to(cum[tm - 1:tm, :], carry.shape)
    lane = lax.broadcasted_iota(jnp.int32, (tm, LANES), 1)
    oa_ref[...] = jnp.where(lane < n_ig, za - cum, za)
    ob_ref[...] = cum
    oc_ref[...] = ls


def _gates(x, w_small, bias_a, bias_b, tri, tiles_per_seq, n_ig):
    n, d = x.shape
    tm = tri.shape[0]
    assert n % tm == 0
    out = jax.ShapeDtypeStruct((n, LANES), F32)
    return pl.pallas_call(
        functools.partial(_gates_kernel, tiles_per_seq=tiles_per_seq, n_ig=n_ig),
        grid=(n // tm,),
        in_specs=[pl.BlockSpec((tm, d), lambda i: (i, 0)),
                  pl.BlockSpec((d, 2 * LANES), lambda i: (0, 0)),
                  pl.BlockSpec((1, LANES), lambda i: (0, 0)),
                  pl.BlockSpec((1, LANES), lambda i: (0, 0)),
                  pl.BlockSpec((tm, tm), lambda i: (0, 0))],
        out_specs=[pl.BlockSpec((tm, LANES), lambda i: (i, 0))] * 3,
        out_shape=[out, out, out],
        scratch_shapes=[pltpu.VMEM((SUBLANES, LANES), F32)],
        compiler_params=_params("arbitrary"),
        name="gates",
    )(x, w_small, bias_a, bias_b, tri)


def _mlstm_kernel(q_ref, k_ref, v_ref, o_ref, ga_ref, gb_ref, arow_ref, c0_ref, n0_ref, m0_ref, g_ref,
                  h_ref, c_ref, n_ref, m_ref, ct_sc, n_sc, a_sc, *, n_heads):
    c = pl.program_id(1)
    L = q_ref.shape[0]
    dk = HEAD_DIM

    @pl.when(c == 0)
    def _():
        ct_sc[...] = c0_ref[...]
        n_sc[...] = jnp.broadcast_to(n0_ref[...], n_sc.shape)
        a_sc[...] = jnp.broadcast_to(m0_ref[...], a_sc.shape)

    lane = lax.broadcasted_iota(jnp.int32, (L, LANES), 1)
    ti = lax.broadcasted_iota(jnp.int32, (L, L), 0)
    si = lax.broadcasted_iota(jnp.int32, (L, L), 1)
    causal = si <= ti
    ga = ga_ref[...]
    gb = gb_ref[...]
    for hd in range(n_heads):
        hs = slice(hd * dk, (hd + 1) * dk)
        q = q_ref[:, hs]
        k = k_ref[:, hs]
        v = v_ref[:, hs]
        sel = lane == hd
        a_col = jnp.sum(jnp.where(sel, ga, 0.0), axis=1, keepdims=True)
        b_col = jnp.sum(jnp.where(sel, gb, 0.0), axis=1, keepdims=True)
        a_row = arow_ref[hd]
        a_prev = a_sc[hd, 0:1, 0:1]
        ct = ct_sc[hd]
        n_row = n_sc[hd, 0:1, :]
        mm = jnp.where(causal, a_row, NEG_INF)
        a_t = jnp.maximum(jnp.max(mm, axis=1, keepdims=True), a_prev)
        s = _dot_nt(q, k) * (dk ** -0.5) * jnp.exp(mm - a_t)
        inter = jnp.exp(a_prev - a_t)
        num = _dot(s.astype(BF16), v) + inter * _dot(q, ct.astype(BF16))
        den = (jnp.sum(s, axis=1, keepdims=True)
               + inter * jnp.sum(q.astype(F32) * n_row, axis=1, keepdims=True))
        h = num / jnp.maximum(jnp.abs(den), jnp.exp(-(b_col + a_t)))
        hm = _sigmoid(o_ref[:, hs].astype(F32)) * h
        hm = hm * lax.rsqrt(jnp.mean(hm * hm, axis=1, keepdims=True) + NORM_EPS) * g_ref[:, hs]
        h_ref[:, hs] = hm.astype(h_ref.dtype)

        a_end = jnp.maximum(jnp.max(a_row, axis=1, keepdims=True), a_prev)
        e_col = jnp.exp(a_col - a_end) * (dk ** -0.5)
        decay = jnp.exp(a_prev - a_end)
        ke = k.astype(F32) * e_col
        ct_new = decay * ct + _dot_tn(ke.astype(BF16), v)
        n_new = decay * n_row + jnp.sum(ke, axis=0, keepdims=True)
        ct_sc[hd] = ct_new
        n_sc[hd] = jnp.broadcast_to(n_new, n_sc.shape[1:])
        a_sc[hd] = jnp.broadcast_to(a_end, a_sc.shape[1:])

        @pl.when(c == pl.num_programs(1) - 1)
        def _():
            c_ref[hd] = ct_new
            n_ref[hd] = n_new
            m_ref[hd] = jnp.broadcast_to(b_col[L - 1:L, :] + a_end, m_ref.shape[1:])


def _mlstm(zm, ga, gb, a_rows, c0t, n0, m0, g_norm, batch, n_heads, seq):
    L = CHUNK
    nc = seq // L
    n = batch * seq
    dh = HEAD_DIM
    hw = n_heads * dh

    def col(off):
        return pl.BlockSpec((L, hw), lambda b, c: (b * nc + c, off))

    def rows():
        return pl.BlockSpec((L, LANES), lambda b, c: (b * nc + c, 0))

    def per_b(r, w):
        return pl.BlockSpec((None, n_heads, r, w), lambda b, c: (b, 0, 0, 0))

    return pl.pallas_call(
        functools.partial(_mlstm_kernel, n_heads=n_heads),
        grid=(batch, nc),
        in_specs=[col(0), col(1), col(2), col(3), rows(), rows(),
                  pl.BlockSpec((None, n_heads, 1, L), lambda b, c: (b, 0, 0, c)),
                  per_b(dh, dh), per_b(1, dh), per_b(1, LANES),
                  pl.BlockSpec((1, hw), lambda b, c: (0, 0))],
        out_specs=[pl.BlockSpec((L, hw), lambda b, c: (b * nc + c, 0)),
                   per_b(dh, dh), per_b(1, dh), per_b(1, LANES)],
        out_shape=[jax.ShapeDtypeStruct((n, hw), BF16),
                   jax.ShapeDtypeStruct((batch, n_heads, dh, dh), F32),
                   jax.ShapeDtypeStruct((batch, n_heads, 1, dh), F32),
                   jax.ShapeDtypeStruct((batch, n_heads, 1, LANES), F32)],
        scratch_shapes=[pltpu.VMEM((n_heads, dh, dh), F32), pltpu.VMEM((n_heads, SUBLANES, dh), F32),
                        pltpu.VMEM((n_heads, SUBLANES, LANES), F32)],
        compiler_params=_params("parallel", "arbitrary"),
        name="mlstm",
    )(zm, zm, zm, zm, ga, gb, a_rows, c0t, n0, m0, g_norm)


def _gla_levels(L):
    levels = []
    w = L // 2
    while w >= 1:
        levels.append(w)
        w //= 2
    return levels


def _gla_consts(L):
    t = np.arange(L)
    row, colj = t[:, None], t[None, :]
    mats = [(colj <= row), (colj > row)]
    masks = []
    for w in _gla_levels(L):
        mid = (t // (2 * w)) * 2 * w + w
        right = t >= mid
        mr = right[:, None] & (colj >= mid[:, None]) & (colj <= row)
        ml = (~right)[:, None] & (colj > row) & (colj < mid[:, None])
        mats.append(mr | ml)
        same = (t[:, None] // (2 * w)) == (t[None, :] // (2 * w))
        masks.append(same & right[:, None] & (~right)[None, :])
    masks.append(row == colj)
    m_all = np.concatenate([m.astype(np.float32) for m in mats], axis=0)
    return m_all, np.stack([m.astype(np.float32) for m in masks])


def _gla_kernel(q_ref, k_ref, v_ref, r_ref, ga_ref, wa_ref, ba_ref, mall_ref, mask_ref, s0_ref, g_ref,
                h_ref, s_ref, st_sc, *, n_valid):
    c = pl.program_id(1)
    L = q_ref.shape[0]
    dk = LANES // 2
    dv = HEAD_DIM
    n_lev = mask_ref.shape[0] - 1
    n_pair = st_sc.shape[0]

    @pl.when(c == 0)
    def _():
        st_sc[...] = s0_ref[...]

    ga = ga_ref[...].astype(BF16)
    mall = mall_ref[...]
    lane = lax.broadcasted_iota(jnp.int32, (1, LANES), 1)
    lm = [(lane < dk).astype(F32), (lane >= dk).astype(F32)]
    for p in range(n_pair):
        ps = slice(p * LANES, (p + 1) * LANES)
        q2 = q_ref[:, ps].astype(F32) * (dk ** -0.5)
        k2 = k_ref[:, ps].astype(F32)
        la = _log_sigmoid(_dot(ga, wa_ref[p]) + ba_ref[p]) * (1.0 / GLA_TAU)
        if n_valid < L:
            valid = lax.broadcasted_iota(jnp.int32, (L, LANES), 0) < n_valid
            la = jnp.where(valid, la, 0.0)
            k2 = jnp.where(valid, k2, 0.0)
        la_hi = la.astype(BF16)
        la_mid = (la - la_hi.astype(F32)).astype(BF16)
        e = jnp.exp(_dot(mall, la_hi) + _dot(mall, la_mid))

        q_in = q2 * e[0:L]
        k_end = (k2 * e[L:2 * L]).astype(BF16)
        decay = e[L - 1:L]
        kq = [(q2 * e[(2 + i) * L:(3 + i) * L], (k2 * e[(2 + i) * L:(3 + i) * L]).astype(BF16))
              for i in range(n_lev)]
        k2b = k2.astype(BF16)
        for hh in range(2):
            hs = slice((2 * p + hh) * dv, (2 * p + hh + 1) * dv)
            v = v_ref[:, hs]
            st = st_sc[p, hh]
            o = _dot_nt((q_in * lm[hh]).astype(BF16), st.astype(BF16))
            a = mask_ref[n_lev] * _dot_nt((q2 * lm[hh]).astype(BF16), k2b)
            for i in range(n_lev):
                qh, kh = kq[i]
                a = a + mask_ref[i] * _dot_nt((qh * lm[hh]).astype(BF16), kh)
            o = o + _dot(a.astype(BF16), v)
            st_sc[p, hh] = decay * st + _dot_tn(v, k_end)
            o = o * lax.rsqrt(jnp.mean(o * o, axis=1, keepdims=True) + NORM_EPS) * g_ref[:, hs]
            r = r_ref[:, hs].astype(F32)
            h_ref[:, hs] = (r * _sigmoid(r) * o).astype(h_ref.dtype)

    @pl.when(c == pl.num_programs(1) - 1)
    def _():
        s_ref[...] = st_sc[...]


def _gla(zg, ga, wa, ba, s0t, g_norm, batch, n_heads, seq, n_valid):
    L = CHUNK
    nc = seq // L
    n = batch * seq
    npair = n_heads // 2
    dv = HEAD_DIM
    m_all, masks = _gla_consts(L)
    m_all = jnp.asarray(m_all, BF16)
    masks = jnp.asarray(masks, F32)
    qkw = npair * LANES
    state = pl.BlockSpec((None, npair, 2, dv, LANES), lambda b, c: (b, 0, 0, 0, 0))

    return pl.pallas_call(
        functools.partial(_gla_kernel, n_valid=n_valid),
        grid=(batch, nc),
        in_specs=[pl.BlockSpec((L, qkw), lambda b, c: (b * nc + c, 0)),
                  pl.BlockSpec((L, qkw), lambda b, c: (b * nc + c, 1)),
                  pl.BlockSpec((L, 2 * qkw), lambda b, c: (b * nc + c, 1)),
                  pl.BlockSpec((L, 2 * qkw), lambda b, c: (b * nc + c, 2)),
                  pl.BlockSpec((L, LANES), lambda b, c: (b * nc + c, 0)),
                  pl.BlockSpec(wa.shape, lambda b, c: (0, 0, 0)),
                  pl.BlockSpec(ba.shape, lambda b, c: (0, 0, 0)),
                  pl.BlockSpec(m_all.shape, lambda b, c: (0, 0)),
                  pl.BlockSpec(masks.shape, lambda b, c: (0, 0, 0)),
                  state,
                  pl.BlockSpec((1, n_heads * dv), lambda b, c: (0, 0))],
        out_specs=[pl.BlockSpec((L, n_heads * dv), lambda b, c: (b * nc + c, 0)), state],
        out_shape=[jax.ShapeDtypeStruct((n, n_heads * dv), BF16),
                   jax.ShapeDtypeStruct((batch, npair, 2, dv, LANES), F32)],
        scratch_shapes=[pltpu.VMEM((npair, 2, dv, LANES), F32)],
        compiler_params=_params("parallel", "arbitrary"),
        name="gla",
    )(zg, zg, zg, zg, ga, wa, ba, m_all, masks, s0t, g_norm)


def _fox_prompt_kernel(qi_ref, ki_ref, q_ref, k_ref, v_ref, gb_ref, crow_ref, o_ref, m_sc, acc_sc, ccol_sc,
                       *, n_heads, lane0):
    g = pl.program_id(0)
    p = pl.program_id(1)
    qi = qi_ref[p]
    ki = ki_ref[p]
    tq = q_ref.shape[0]
    tk = k_ref.shape[0]
    dh = HEAD_DIM
    hpg = q_ref.shape[1] // dh
    groups_per_batch = n_heads // hpg

    @pl.when(ki == 0)
    def _():
        m_sc[...] = jnp.full_like(m_sc, NEG_INF)
        acc_sc[...] = jnp.zeros_like(acc_sc)
        lane = lax.broadcasted_iota(jnp.int32, (tq, LANES), 1)
        gb = gb_ref[...]
        for hh in range(hpg):
            hd = (g % groups_per_batch) * hpg + hh
            ccol_sc[hh] = LOG2E * jnp.sum(jnp.where(lane == lane0 + hd, gb, 0.0), axis=1, keepdims=True)

    def step(diagonal):
        for hh in range(hpg):
            hs = slice(hh * dh, (hh + 1) * dh)
            u = _dot_nt(q_ref[:, hs], k_ref[:, hs]) - LOG2E * crow_ref[hh]
            if diagonal:
                ti = lax.broadcasted_iota(jnp.int32, (tq, tk), 0)
                si = lax.broadcasted_iota(jnp.int32, (tq, tk), 1)
                u = jnp.where(si <= ti, u, NEG_INF)
            c2 = ccol_sc[hh]
            m_old = m_sc[hh]
            m_new = jnp.maximum(m_old, jnp.max(u, axis=1, keepdims=True) + c2)
            pr = jnp.exp2(u - (m_new - c2))
            v1 = jnp.concatenate([v_ref[:, hs], jnp.ones((tk, LANES), BF16)], axis=1)
            acc_sc[hh] = jnp.exp2(m_old - m_new) * acc_sc[hh] + _dot(pr.astype(BF16), v1)
            m_sc[hh] = m_new

    @pl.when(ki < qi)
    def _():
        step(False)

    @pl.when(ki == qi)
    def _():
        step(True)
        for hh in range(hpg):
            acc = acc_sc[hh]
            o_ref[:, hh * dh:(hh + 1) * dh] = (acc[:, :dh] / acc[:, dh:]).astype(o_ref.dtype)


def _fox_prompt(zq, zk, zv, gb, c_rows, batch, n_heads, seq, tile, lane0):
    n = batch * seq
    nq = seq // tile
    dh = HEAD_DIM
    hpg = ATTN_HEADS_PER_STEP
    gpb = n_heads // hpg
    pairs = [(a, b) for a in range(nq) for b in range(a + 1)]
    qi = jnp.asarray([a for a, _ in pairs], jnp.int32)
    ki = jnp.asarray([b for _, b in pairs], jnp.int32)
    c_rows = c_rows.reshape(batch * gpb, hpg, 1, seq)

    def qrow(g, p, qi, ki):
        return (g // gpb) * nq + qi[p]

    def krow(g, p, qi, ki):
        return (g // gpb) * nq + ki[p]

    grid_spec = pltpu.PrefetchScalarGridSpec(
        num_scalar_prefetch=2,
        grid=(batch * gpb, len(pairs)),
        in_specs=[pl.BlockSpec((tile, hpg * dh), lambda g, p, qi, ki: (qrow(g, p, qi, ki), g % gpb)),
                  pl.BlockSpec((tile, hpg * dh), lambda g, p, qi, ki: (krow(g, p, qi, ki), g % gpb)),
                  pl.BlockSpec((tile, hpg * dh), lambda g, p, qi, ki: (krow(g, p, qi, ki), g % gpb)),
                  pl.BlockSpec((tile, LANES), lambda g, p, qi, ki: (qrow(g, p, qi, ki), 0)),
                  pl.BlockSpec((None, hpg, 1, tile), lambda g, p, qi, ki: (g, 0, 0, ki[p]))],
        out_specs=pl.BlockSpec((tile, hpg * dh), lambda g, p, qi, ki: (qrow(g, p, qi, ki), g % gpb)),
        scratch_shapes=[pltpu.VMEM((hpg, tile, 1), F32), pltpu.VMEM((hpg, tile, dh + LANES), F32),
                        pltpu.VMEM((hpg, tile, 1), F32)],
    )
    return pl.pallas_call(
        functools.partial(_fox_prompt_kernel, n_heads=n_heads, lane0=lane0),
        grid_spec=grid_spec,
        out_shape=jax.ShapeDtypeStruct((n, n_heads * dh), BF16),
        compiler_params=_params("parallel", "arbitrary"),
        name="fox_prompt",
    )(qi, ki, zq, zk, zv, gb, c_rows)


def _fox_sample_kernel(pt_ref, q_ref, cq_ref, kn_ref, vn_ref, bn_ref, *rest, n_heads, n_new, group):
    k_refs, v_refs, lf_refs = rest[:group], rest[group:2 * group], rest[2 * group:3 * group]
    o_ref, m_sc, l_sc, acc_sc, carry_sc = rest[3 * group:]
    j = pl.program_id(1)
    nr = q_ref.shape[0]
    rows_pp = k_refs[0].shape[0]
    d = q_ref.shape[1]
    n_tiles = rows_pp // LANES
    scale = d ** -0.5

    @pl.when(j == 0)
    def _():
        m_sc[...] = jnp.full_like(m_sc, NEG_INF)
        l_sc[...] = jnp.zeros_like(l_sc)
        acc_sc[...] = jnp.zeros_like(acc_sc)
        carry_sc[...] = jnp.zeros_like(carry_sc)

    lane8 = lax.broadcasted_iota(jnp.int32, (SUBLANES, LANES), 1)
    row8 = lax.broadcasted_iota(jnp.int32, (SUBLANES, LANES), 0)

    def page_suffix(x):
        y = x
        z = x
        sh = n_heads
        while sh < LANES:
            y = y + jnp.where(lane8 + sh < LANES, pltpu.roll(y, LANES - sh, 1), 0.0)
            z = z + pltpu.roll(z, sh, 1)
            sh *= 2
        w = z
        sh = 1
        while sh < SUBLANES:
            w = w + jnp.where(row8 + sh < SUBLANES, pltpu.roll(w, SUBLANES - sh, 0), 0.0)
            sh *= 2
        return y - x + (w - z), jnp.broadcast_to(w[0:1, :], x.shape)

    q = q_ref[...]
    cq = cq_ref[...]
    rowi = lax.broadcasted_iota(jnp.int32, (nr, LANES), 0)
    lanei = lax.broadcasted_iota(jnp.int32, (nr, LANES), 1)
    head_ok = (rowi // n_new) == (lanei % n_heads)

    def update(tile_groups, vbs):
        flat = [t for ts in tile_groups for t in ts]
        m_old = m_sc[...]
        mx = flat[0]
        for t in flat[1:]:
            mx = jnp.maximum(mx, t)
        m_new = jnp.maximum(m_old, jnp.max(mx, axis=1, keepdims=True))
        alpha = jnp.exp(m_old - m_new)
        tot = None
        acc = alpha * acc_sc[...]
        for ts, vb in zip(tile_groups, vbs):
            ps = [jnp.exp(t - m_new) for t in ts]
            for t in ps:
                tot = t if tot is None else tot + t
            pcat = ps[0] if len(ps) == 1 else jnp.concatenate(ps, axis=1)
            acc = acc + _dot(pcat.astype(BF16), vb)
        l_sc[...] = alpha * l_sc[...] + jnp.sum(tot, axis=1, keepdims=True)
        acc_sc[...] = acc
        m_sc[...] = m_new

    carry = carry_sc[...]
    tile_groups = []
    for g in range(group):
        within, total = page_suffix(lf_refs[g][...])
        suffix = within + carry
        carry = carry + total
        s = _dot_nt(q, k_refs[g][...].astype(BF16)) * scale
        tile_groups.append([jnp.where(head_ok, s[:, r * LANES:(r + 1) * LANES] + cq + suffix[r:r + 1, :], NEG_INF)
                            for r in range(n_tiles)])
    carry_sc[...] = carry
    update(tile_groups, [v_refs[g][...].astype(BF16) for g in range(group)])

    @pl.when(j == pl.num_programs(1) - 1)
    def _():
        s_self = _dot_nt(q, kn_ref[...]) * scale + cq + bn_ref[...]
        ok = head_ok & (lanei < n_new * n_heads) & ((lanei // n_heads) <= (rowi % n_new))
        update([[jnp.where(ok, s_self, NEG_INF)]], [vn_ref[...]])
        o_ref[...] = acc_sc[...] / l_sc[...]


def _fox_sample(page_table, q, cq, kn, vn, bn, cache_k, cache_v, cache_lf, layer, n_heads, n_new):
    nb, n_pages = page_table.shape
    nr = q.shape[1]
    rows_pp = cache_k.shape[2]
    d = q.shape[2]
    group = max(g for g in (1, 2, 4, 8) if n_pages % g == 0)

    def page_spec(rows, width, g):
        return pl.BlockSpec((None, None, rows, width),
                            lambda b, j, pt: (layer, pt[b, n_pages - 1 - (j * group + g)], 0, 0))

    per_b = lambda rows, width: pl.BlockSpec((None, rows, width), lambda b, j, pt: (b, 0, 0))
    grid_spec = pltpu.PrefetchScalarGridSpec(
        num_scalar_prefetch=1,
        grid=(nb, n_pages // group),
        in_specs=([per_b(nr, d), per_b(nr, LANES), per_b(LANES, d), per_b(LANES, d), per_b(1, LANES)]
                  + [page_spec(rows_pp, d, g) for g in range(group)]
                  + [page_spec(rows_pp, d, g) for g in range(group)]
                  + [page_spec(SUBLANES, LANES, g) for g in range(group)]),
        out_specs=per_b(nr, d),
        scratch_shapes=[pltpu.VMEM((nr, 1), F32), pltpu.VMEM((nr, 1), F32), pltpu.VMEM((nr, d), F32),
                        pltpu.VMEM((SUBLANES, LANES), F32)],
    )
    return pl.pallas_call(
        functools.partial(_fox_sample_kernel, n_heads=n_heads, n_new=n_new, group=group),
        grid_spec=grid_spec,
        out_shape=jax.ShapeDtypeStruct((nb, nr, d), F32),
        compiler_params=_params("parallel", "arbitrary"),
        name="fox_sample",
    )(page_table, q, cq, kn, vn, bn, *([cache_k] * group), *([cache_v] * group), *([cache_lf] * group))


def _layer_norm(r, g, b):
    mu = jnp.mean(r, axis=1, keepdims=True)
    xc = r - mu
    var = jnp.mean(xc * xc, axis=1, keepdims=True)
    return xc * lax.rsqrt(var + LN_EPS) * g + b


def _outproj_kernel(hm_ref, hg_ref, hf_ref, wm_ref, wg_ref, wf_ref, x_ref, g_ref, b_ref, h_ref, hb_ref, *, alpha):
    mix = _dot(hm_ref[...], wm_ref[...]) + _dot(hg_ref[...], wg_ref[...]) + _dot(hf_ref[...], wf_ref[...])
    h = _layer_norm(alpha * x_ref[...] + mix, g_ref[...], b_ref[...])
    h_ref[...] = h
    hb_ref[...] = h.astype(BF16)


def _outproj_ln(hm, hg, hf, w_out, x, g, b, alpha, tm):
    n, d = x.shape
    tm = min(tm, n)
    wm, wg, wf = hm.shape[1], hg.shape[1], hf.shape[1]
    assert wm == wg and wf == wm + wg and n % tm == 0
    const = pl.Buffered(1)
    return pl.pallas_call(
        functools.partial(_outproj_kernel, alpha=alpha),
        grid=(n // tm,),
        in_specs=[pl.BlockSpec((tm, wm), lambda i: (i, 0)),
                  pl.BlockSpec((tm, wg), lambda i: (i, 0)),
                  pl.BlockSpec((tm, wf), lambda i: (i, 0)),
                  pl.BlockSpec((wm, d), lambda i: (0, 0), pipeline_mode=const),
                  pl.BlockSpec((wg, d), lambda i: (1, 0), pipeline_mode=const),
                  pl.BlockSpec((wf, d), lambda i: (1, 0), pipeline_mode=const),
                  pl.BlockSpec((tm, d), lambda i: (i, 0)),
                  pl.BlockSpec((1, d), lambda i: (0, 0)),
                  pl.BlockSpec((1, d), lambda i: (0, 0))],
        out_specs=[pl.BlockSpec((tm, d), lambda i: (i, 0))] * 2,
        out_shape=[jax.ShapeDtypeStruct((n, d), F32), jax.ShapeDtypeStruct((n, d), BF16)],
        compiler_params=_params("parallel"),
        name="outproj_ln",
    )(hm, hg, hf, w_out, w_out, w_out, x, g, b)


def _ffn_up_kernel(x_ref, wg_ref, wv_ref, cwg_ref, cwv_ref, cbg_ref, cbv_ref, hg0_ref, hv0_ref,
                   o_ref, tg_ref, tv_ref, halo_g, halo_v, win_g, win_v, *, tiles_per_seq, shift):
    i = pl.program_id(0)
    j = pl.program_id(1)
    tm = x_ref.shape[0]
    hr = hg0_ref.shape[0]
    tn = o_ref.shape[1]

    @pl.when(i % tiles_per_seq == 0)
    def _():
        halo_g[j] = hg0_ref[...]
        halo_v[j] = hv0_ref[...]

    rc = min(FFN_ROW_CHUNK, tm)

    rb = min(FFN_EPILOGUE_ROWS, rc)

    def conv(r0, cs, cw_ref, cb_ref, win):
        cw = cw_ref[:, cs]
        return (cb_ref[:, cs] + cw[0:1] * win[pl.ds(hr + r0 - 2 * shift, rb), cs]
                + cw[1:2] * win[pl.ds(hr + r0 - shift, rb), cs] + cw[2:3] * win[pl.ds(hr + r0, rb), cs])

    win_g[0:hr, :] = halo_g[j]
    win_v[0:hr, :] = halo_v[j]
    for c0 in range(0, tn, FFN_COL_SLAB):
        cs = slice(c0, c0 + FFN_COL_SLAB)
        for r0 in range(0, tm, rc):
            rs = slice(r0, r0 + rc)
            ug = _dot(x_ref[rs, :], wg_ref[:, cs])
            uv = _dot(x_ref[rs, :], wv_ref[:, cs])
            win_g[hr + r0:hr + r0 + rc, cs] = ug
            win_v[hr + r0:hr + r0 + rc, cs] = uv
            for r1 in range(r0, r0 + rc, rb):
                yg = conv(r1, cs, cwg_ref, cbg_ref, win_g)
                yv = conv(r1, cs, cwv_ref, cbv_ref, win_v)
                o_ref[r1:r1 + rb, cs] = (yg * _sigmoid(yg) * yv).astype(o_ref.dtype)
    tail_g = win_g[tm:tm + hr, :]
    tail_v = win_v[tm:tm + hr, :]
    tg_ref[...] = tail_g
    tv_ref[...] = tail_v
    halo_g[j] = tail_g
    halo_v[j] = tail_v


def _ffn_up(x, w_up, conv_w, conv_b, halo0, fp, tm, tn, tiles_per_seq, shift):
    n, d = x.shape
    tm = min(tm, n)
    nj = fp // tn
    ni = n // tm
    hr = halo0.shape[0] // (ni // tiles_per_seq)
    assert n % tm == 0 and fp % tn == 0 and hr >= 2 * shift
    seq = lambda i: i // tiles_per_seq
    return pl.pallas_call(
        functools.partial(_ffn_up_kernel, tiles_per_seq=tiles_per_seq, shift=shift),
        grid=(ni, nj),
        in_specs=[pl.BlockSpec((tm, d), lambda i, j: (i, 0)),
                  pl.BlockSpec((d, tn), lambda i, j: (0, j)),
                  pl.BlockSpec((d, tn), lambda i, j: (0, nj + j)),
                  pl.BlockSpec((CONV_W, tn), lambda i, j: (0, j)),
                  pl.BlockSpec((CONV_W, tn), lambda i, j: (0, nj + j)),
                  pl.BlockSpec((1, tn), lambda i, j: (0, j)),
                  pl.BlockSpec((1, tn), lambda i, j: (0, nj + j)),
                  pl.BlockSpec((hr, tn), lambda i, j: (seq(i), j)),
                  pl.BlockSpec((hr, tn), lambda i, j: (seq(i), nj + j))],
        out_specs=[pl.BlockSpec((tm, tn), lambda i, j: (i, j)),
                   pl.BlockSpec((hr, tn), lambda i, j: (i, j)),
                   pl.BlockSpec((hr, tn), lambda i, j: (i, j))],
        out_shape=[jax.ShapeDtypeStruct((n, fp), BF16),
                   jax.ShapeDtypeStruct((ni * hr, fp), F32),
                   jax.ShapeDtypeStruct((ni * hr, fp), F32)],
        scratch_shapes=[pltpu.VMEM((nj, hr, tn), F32), pltpu.VMEM((nj, hr, tn), F32),
                        pltpu.VMEM((hr + tm, tn), F32), pltpu.VMEM((hr + tm, tn), F32)],
        compiler_params=_params("arbitrary", "arbitrary"),
        name="ffn_up",
    )(x, w_up, w_up, conv_w, conv_w, conv_b, conv_b, halo0, halo0)


def _ffn_down_kernel(a_ref, w_ref, h_ref, g_ref, b_ref, x_ref, xb_ref, *, alpha):
    x = _layer_norm(alpha * h_ref[...] + _dot(a_ref[...], w_ref[...]), g_ref[...], b_ref[...])
    x_ref[...] = x
    xb_ref[...] = x.astype(BF16)


def _ffn_down_ln(a, w_down, h, g, b, alpha, tm):
    n, fp = a.shape
    d = w_down.shape[1]
    tm = min(tm, n)
    assert n % tm == 0
    return pl.pallas_call(
        functools.partial(_ffn_down_kernel, alpha=alpha),
        grid=(n // tm,),
        in_specs=[pl.BlockSpec((tm, fp), lambda i: (i, 0)),
                  pl.BlockSpec((fp, d), lambda i: (0, 0), pipeline_mode=pl.Buffered(1)),
                  pl.BlockSpec((tm, d), lambda i: (i, 0)),
                  pl.BlockSpec((1, d), lambda i: (0, 0)),
                  pl.BlockSpec((1, d), lambda i: (0, 0))],
        out_specs=[pl.BlockSpec((tm, d), lambda i: (i, 0))] * 2,
        out_shape=[jax.ShapeDtypeStruct((n, d), F32), jax.ShapeDtypeStruct((n, d), BF16)],
        compiler_params=_params("parallel"),
        name="ffn_down_ln",
    )(a, w_down, h, g, b)


def _prep_layer(l, dims, w_in, b_m_ig, b_m_fg, w_g_alpha_up, b_g_alpha, b_f, g_m_norm, g_g_norm, w_out,
                ln1_g, ln1_b, w_up, conv_w, conv_b, w_down, ln2_g, ln2_b):
    d, hm, hg, hf, f, fp = dims
    dh = HEAD_DIM
    dkg = dh // 2
    sizes = [hm * dh] * 4 + [hm, hm] + [hg * dkg] * 2 + [hg * dh] * 2 + [GLA_RANK] + [hf * dh] * 3 + [hf]
    offs = np.concatenate([[0], np.cumsum(sizes)]).tolist()
    (o_mq, o_mk, o_mv, o_mo, o_mi, o_mf, o_gq, o_gk, o_gv, o_gr, o_ga, o_fq, o_fk, o_fv, o_ff, _) = offs
    w = w_in[l]
    col = lambda a, b: w[:, a:b]
    zeros = lambda c: jnp.zeros((d, c), F32)
    blk_a = jnp.concatenate([col(o_mi, o_mi + hm), zeros(8 - hm), col(o_ga, o_ga + GLA_RANK),
                             zeros(LANES - 8 - GLA_RANK)], axis=1)
    blk_b = jnp.concatenate([col(o_mf, o_mf + hm), zeros(8 - hm), col(o_ff, o_ff + hf),
                             zeros(LANES - 8 - hf)], axis=1)
    pad1 = lambda v, lo, total: jnp.pad(v, (lo, total - lo - v.shape[0]))[None, :]
    wa = jnp.zeros((LANES, hg * dkg), F32).at[8:8 + GLA_RANK].set(w_g_alpha_up[l])
    npair = hg // 2
    pad_f = lambda a: jnp.pad(a, [(0, 0)] * (a.ndim - 1) + [(0, fp - f)])
    w_m, w_g, w_fq, w_fk, w_fv = _cast_groups(
        w_in, l, [(o_mq, o_mi), (o_gq, o_ga), (o_fq, o_fk), (o_fk, o_fv), (o_fv, o_ff)], 256)
    return dict(
        w_m=w_m, w_g=w_g, w_fq=w_fq, w_fk=w_fk, w_fv=w_fv,
        w_small=jnp.concatenate([blk_a, blk_b], axis=1).astype(BF16),
        bias_a=pad1(b_m_ig[l], 0, LANES),
        bias_b=pad1(b_m_fg[l], 0, LANES) + pad1(b_f[l], 8, LANES),
        wa=wa.reshape(LANES, npair, LANES).transpose(1, 0, 2).astype(BF16),
        ba=b_g_alpha[l].reshape(npair, 1, LANES),
        g_m=g_m_norm[l][None, :], g_g=g_g_norm[l][None, :],
        w_out=w_out[l].astype(BF16),
        ln1_g=ln1_g[l][None, :], ln1_b=ln1_b[l][None, :], ln2_g=ln2_g[l][None, :], ln2_b=ln2_b[l][None, :],
        w_up=_cast_pad_cols(w_up, l, 2, fp, 256),
        conv_w=jnp.concatenate([pad_f(conv_w[l][:, :f]), pad_f(conv_w[l][:, f:])], axis=1),
        conv_b=jnp.concatenate([pad_f(conv_b[l][:f]), pad_f(conv_b[l][f:])])[None, :],
        w_down=_cast_pad_rows(w_down, l, fp, 512),
    )


def _pad_conv_state(s, f, fp):
    pad = [(0, 0)] * (s.ndim - 1) + [(0, fp - f)]
    return jnp.concatenate([jnp.pad(s[..., :f], pad), jnp.pad(s[..., f:], pad)], axis=-1)


def _mixer_rows(x_bf, p, tm, q_scale=1.0):
    zm, = _matmul(x_bf, p["w_m"], [BF16], tm, 512, "proj_m")
    zg, = _matmul(x_bf, p["w_g"], [BF16], tm, 512, "proj_g")
    zq, = _matmul(x_bf, p["w_fq"], [BF16], tm, 512, "proj_fq", scale=q_scale)
    zk, zkb = _matmul(x_bf, p["w_fk"], [F32, BF16], tm, 512, "proj_fk")
    zv, zvb = _matmul(x_bf, p["w_fv"], [F32, BF16], tm, 512, "proj_fv")
    return zm, zg, zq, zk, zkb, zv, zvb


def kernel(x_prompt, x_sample, state_mlstm_c, state_mlstm_n, state_mlstm_m, state_gla, state_ffn_conv,
           cache_k, cache_v, cache_logf, page_table,
           w_in, b_m_ig, b_m_fg, w_g_alpha_up, b_g_alpha, b_f, g_m_norm, g_g_norm, w_out,
           ln1_g, ln1_b, w_up, conv_w, conv_b, w_down, ln2_g, ln2_b):
    bp, seq, d = x_prompt.shape
    db, ns, _ = x_sample.shape
    depth = w_in.shape[0]
    hm = b_m_ig.shape[1]
    hf = b_f.shape[1]
    dh = HEAD_DIM
    hg = g_g_norm.shape[1] // dh
    dkg = dh // 2
    f = w_down.shape[1]
    fp = -(-f // 512) * 512
    alpha = (2.0 * depth) ** 0.25
    n_pool, page = cache_k.shape[1], cache_k.shape[2]
    assert hm <= 8 and hf <= 8 and hg % 2 == 0 and seq % 512 == 0 and ns <= CHUNK
    assert page * hf == SUBLANES * LANES and ns * hf <= LANES
    dims = (d, hm, hg, hf, f, fp)
    L = CHUNK
    n_p = bp * seq

    tri_p = jnp.asarray(np.tril(np.ones((GATE_TILE, GATE_TILE), np.float32)), BF16)
    gs_rows = LANES
    r = np.arange(gs_rows)
    tri_s_np = ((r[:, None] % db == r[None, :] % db) & (r[None, :] // db <= r[:, None] // db)
                & (r[:, None] < ns * db) & (r[None, :] < ns * db))
    tri_s = jnp.asarray(tri_s_np.astype(np.float32), BF16)

    xp = x_prompt.astype(F32).reshape(n_p, d)
    xp_bf = xp.astype(BF16)
    xs = x_sample.astype(F32).transpose(1, 0, 2).reshape(ns * db, d)
    xs_bf = xs.astype(BF16)

    ck = cache_k.astype(F32).reshape(depth, n_pool, page * hf, dh)
    cv = cache_v.astype(F32).reshape(depth, n_pool, page * hf, dh)
    clf = cache_logf.astype(F32).reshape(depth, n_pool, SUBLANES, LANES)

    def to_padded(z, pad_value=0.0, mode="constant"):
        c = z.shape[-1]
        a = z.reshape(ns, db, c).transpose(1, 0, 2)
        if mode == "edge":
            a = jnp.pad(a, ((0, 0), (0, L - ns), (0, 0)), mode="edge")
        else:
            a = jnp.pad(a, ((0, 0), (0, L - ns), (0, 0)), constant_values=pad_value)
        return a.reshape(db * L, c)

    def from_padded(y):
        c = y.shape[-1]
        return y.reshape(db, L, c)[:, :ns].transpose(1, 0, 2).reshape(ns * db, c)

    def head_rows(g, lane0, nh, batch, t):
        return g[:, lane0:lane0 + nh].reshape(batch, t, nh).transpose(0, 2, 1).reshape(batch * nh, 1, t)

    outs_p = [[] for _ in range(8)]
    outs_s = [[] for _ in range(8)]
    for l in range(depth):
        p = _prep_layer(l, dims, w_in, b_m_ig, b_m_fg, w_g_alpha_up, b_g_alpha, b_f, g_m_norm, g_g_norm, w_out,
                        ln1_g, ln1_b, w_up, conv_w, conv_b, w_down, ln2_g, ln2_b)

        zm, zg, zq, zk, zkb, zv, zvb = _mixer_rows(xp_bf, p, 1024, q_scale=LOG2E * dh ** -0.5)
        ga, gb, gc = _gates(xp_bf, p["w_small"], p["bias_a"], p["bias_b"], tri_p, seq // GATE_TILE, hm)
        zeros = lambda *s: jnp.zeros(s, F32)
        h_m, ct, n_m, m_m = _mlstm(zm, ga, gb, head_rows(ga, 0, hm, bp, seq).reshape(bp, hm, 1, seq),
                                   zeros(bp, hm, dh, dh), zeros(bp, hm, 1, dh), zeros(bp, hm, 1, LANES),
                                   p["g_m"], bp, hm, seq)
        h_g, st = _gla(zg, ga, p["wa"], p["ba"], zeros(bp, hg // 2, 2, dh, LANES), p["g_g"], bp, hg, seq, L)
        h_f = _fox_prompt(zq, zkb, zvb, gb, head_rows(gb, 8, hf, bp, seq), bp, hf, seq, ATTN_TILE, 8)
        h, h_bf = _outproj_ln(h_m, h_g, h_f, p["w_out"], xp, p["ln1_g"], p["ln1_b"], alpha, 512)
        tm_f = min(FFN_ROW_TILE, seq)
        act, tg, tv = _ffn_up(h_bf, p["w_up"], p["conv_w"], p["conv_b"], zeros(bp * SUBLANES, 2 * fp),
                              fp, tm_f, 512, seq // tm_f, 1)
        xp, xp_bf = _ffn_down_ln(act, p["w_down"], h, p["ln2_g"], p["ln2_b"], alpha, FFN_DOWN_ROW_TILE)

        def conv_tail_p(t):
            t = t.reshape(bp, seq // tm_f, SUBLANES, fp)[:, -1, SUBLANES - (CONV_W - 1):, :f]
            return t
        outs_p[0].append(ct.reshape(bp, hm, dh, dh).swapaxes(-1, -2))
        outs_p[1].append(n_m.reshape(bp, hm, dh))
        outs_p[2].append(m_m[:, :, 0, 0])
        st_h = st.reshape(bp, hg // 2, 2, dh, 2, dkg)
        st_h = jnp.stack([st_h[:, :, 0, :, 0, :], st_h[:, :, 1, :, 1, :]], axis=2)
        outs_p[3].append(st_h.reshape(bp, hg, dh, dkg).swapaxes(-1, -2))
        outs_p[4].append(jnp.concatenate([conv_tail_p(tg), conv_tail_p(tv)], axis=-1))
        outs_p[5].append(zk.reshape(bp, seq, hf, dh))
        outs_p[6].append(zv.reshape(bp, seq, hf, dh))
        outs_p[7].append(gc[:, 8:8 + hf].reshape(bp, seq, hf))

        n_s = ns * db
        zm, zg, zq, zk, zkb, zv, zvb = _mixer_rows(xs_bf, p, n_s)
        xs_pad = jnp.pad(xs_bf, ((0, gs_rows - n_s), (0, 0)))
        ga, gb, gc = _gates(xs_pad, p["w_small"], p["bias_a"], p["bias_b"], tri_s, 1, hm)
        ga, gb, gc = ga[:n_s], gb[:n_s], gc[:n_s]
        ga_p = to_padded(ga, NEG_INF)
        gb_p = to_padded(gb, mode="edge")
        c0t = state_mlstm_c[l].astype(F32).swapaxes(-1, -2)
        n0 = state_mlstm_n[l].astype(F32).reshape(db, hm, 1, dh)
        m0 = jnp.broadcast_to(state_mlstm_m[l].astype(F32).reshape(db, hm, 1, 1), (db, hm, 1, LANES))
        h_m, ct, n_m, m_m = _mlstm(to_padded(zm), ga_p, gb_p, head_rows(ga_p, 0, hm, db, L).reshape(db, hm, 1, L),
                                   c0t, n0, m0, p["g_m"], db, hm, L)
        s0 = state_gla[l].astype(F32).swapaxes(-1, -2).reshape(db, hg // 2, 2, dh, dkg)
        s0t = jnp.stack([jnp.pad(s0[:, :, 0], ((0, 0), (0, 0), (0, 0), (0, dkg))),
                         jnp.pad(s0[:, :, 1], ((0, 0), (0, 0), (0, 0), (dkg, 0)))], axis=2)
        h_g, st = _gla(to_padded(zg), to_padded(ga), p["wa"], p["ba"], s0t, p["g_g"], db, hg, L, ns)

        q_s = zq.reshape(ns, db, hf, dh).transpose(1, 2, 0, 3).reshape(db, hf * ns, dh)
        new_rows = lambda z: jnp.pad(z.reshape(ns, db, hf * dh).transpose(1, 0, 2).reshape(db, ns * hf, dh),
                                     ((0, 0), (0, LANES - ns * hf), (0, 0)))
        c_new = gb[:, 8:8 + hf].reshape(ns, db, hf)
        cq = jnp.broadcast_to(c_new.transpose(1, 2, 0).reshape(db, hf * ns, 1), (db, hf * ns, LANES))
        bn = jnp.pad(-c_new.transpose(1, 0, 2).reshape(db, 1, ns * hf), ((0, 0), (0, 0), (0, LANES - ns * hf)))
        o_f = _fox_sample(page_table, q_s, cq, new_rows(zkb), new_rows(zvb), bn, ck, cv, clf, l, hf, ns)
        h_f = o_f.reshape(db, hf, ns, dh).transpose(2, 0, 1, 3).reshape(n_s, hf * dh).astype(BF16)

        h, h_bf = _outproj_ln(from_padded(h_m), from_padded(h_g), h_f, p["w_out"], xs, p["ln1_g"], p["ln1_b"],
                              alpha, n_s)
        halo_s = _pad_conv_state(state_ffn_conv[l].astype(F32).transpose(1, 0, 2).reshape((CONV_W - 1) * db, 2 * f),
                                 f, fp)
        act, tg, tv = _ffn_up(h_bf, p["w_up"], p["conv_w"], p["conv_b"], halo_s, fp, n_s, 512, 1, db)
        xs, xs_bf = _ffn_down_ln(act, p["w_down"], h, p["ln2_g"], p["ln2_b"], alpha, n_s)

        conv_tail_s = lambda t: t[:, :f].reshape(CONV_W - 1, db, f).transpose(1, 0, 2)
        outs_s[0].append(ct.reshape(db, hm, dh, dh).swapaxes(-1, -2))
        outs_s[1].append(n_m.reshape(db, hm, dh))
        outs_s[2].append(m_m[:, :, 0, 0])
        st_h = st.reshape(db, hg // 2, 2, dh, 2, dkg)
        st_h = jnp.stack([st_h[:, :, 0, :, 0, :], st_h[:, :, 1, :, 1, :]], axis=2)
        outs_s[3].append(st_h.reshape(db, hg, dh, dkg).swapaxes(-1, -2))
        outs_s[4].append(jnp.concatenate([conv_tail_s(tg), conv_tail_s(tv)], axis=-1))
        outs_s[5].append(zk.reshape(ns, db, hf, dh).transpose(1, 0, 2, 3))
        outs_s[6].append(zv.reshape(ns, db, hf, dh).transpose(1, 0, 2, 3))
        outs_s[7].append(gc[:, 8:8 + hf].reshape(ns, db, hf).transpose(1, 0, 2))

    y_p = xp.reshape(bp, seq, d)
    y_s = xs.reshape(ns, db, d).transpose(1, 0, 2)
    return (y_p, y_s) + tuple(jnp.stack(a) for a in outs_p) + tuple(jnp.stack(a) for a in outs_s)
```

```python
import functools

import numpy as np
import jax
import jax.numpy as jnp
from jax import lax
from jax.experimental import pallas as pl
from jax.experimental.pallas import tpu as pltpu

F32 = jnp.float32
BF16 = jnp.bfloat16

HEAD_DIM = 128
GLA_RANK = 16
GLA_TAU = 16.0
CONV_W = 3
LN_EPS = 1e-5
NORM_EPS = 1e-6

LANES = 128
SUBLANES = 8
VMEM_LIMIT = 56 * 1024 * 1024

CHUNK = 128
GATE_TILE = 256
ATTN_TILE = 512
ATTN_HEADS_PER_STEP = 8
FFN_ROW_TILE = 1024
FFN_COL_SLAB = 256
FFN_ROW_CHUNK = 512
FFN_EPILOGUE_ROWS = 512
FFN_DOWN_ROW_TILE = 256
NEG_INF = float("-inf")
LOG2E = 1.4426950408889634


def _params(*sem, flags=None):
    return pltpu.CompilerParams(dimension_semantics=sem, vmem_limit_bytes=VMEM_LIMIT, flags=flags)


def _dot(a, b):
    return jnp.dot(a, b, preferred_element_type=F32)


def _dot_nt(a, b):
    return lax.dot_general(a, b, (((1,), (1,)), ((), ())), preferred_element_type=F32)


def _dot_tn(a, b):
    return lax.dot_general(a, b, (((0,), (0,)), ((), ())), preferred_element_type=F32)


def _log_sigmoid(x):
    return jnp.minimum(x, 0.0) - jnp.log1p(jnp.exp(-jnp.abs(x)))


def _sigmoid(x):
    return 1.0 / (1.0 + jnp.exp(-x))


def _split3(x):
    hi = x.astype(BF16)
    r = x - hi.astype(F32)
    mid = r.astype(BF16)
    lo = (r - mid.astype(F32)).astype(BF16)
    return hi, mid, lo


def _mm_kernel(x_ref, w_ref, *o_refs, scale):
    acc = _dot(x_ref[...], w_ref[...])
    if scale != 1.0:
        acc = acc * scale
    for o in o_refs:
        o[...] = acc.astype(o.dtype)


def _matmul(x, w, out_dtypes, tm, tn, name, scale=1.0):
    m, k = x.shape
    n = w.shape[1]
    tm = min(tm, m)
    tn = min(tn, n)
    assert m % tm == 0 and n % tn == 0
    return pl.pallas_call(
        functools.partial(_mm_kernel, scale=scale),
        grid=(m // tm, n // tn),
        in_specs=[pl.BlockSpec((tm, k), lambda i, j: (i, 0)),
                  pl.BlockSpec((k, tn), lambda i, j: (0, j))],
        out_specs=[pl.BlockSpec((tm, tn), lambda i, j: (i, j)) for _ in out_dtypes],
        out_shape=[jax.ShapeDtypeStruct((m, n), d) for d in out_dtypes],
        compiler_params=_params("parallel", "parallel"),
        name=name,
    )(x, w)


def _cast_pad_rows_kernel(x_ref, o_ref):
    r = x_ref.shape[0]
    o_ref[0:r, :] = x_ref[...].astype(o_ref.dtype)
    if o_ref.shape[0] > r:
        o_ref[r:, :] = jnp.zeros((o_ref.shape[0] - r, o_ref.shape[1]), o_ref.dtype)


def _cast_pad_rows(w, layer, rows_padded, tc):
    _, r, c = w.shape
    assert r % 16 == 0 and rows_padded % 16 == 0 and c % tc == 0
    return pl.pallas_call(
        _cast_pad_rows_kernel,
        grid=(c // tc,),
        in_specs=[pl.BlockSpec((None, r, tc), lambda j: (layer, 0, j))],
        out_specs=pl.BlockSpec((rows_padded, tc), lambda j: (0, j)),
        out_shape=jax.ShapeDtypeStruct((rows_padded, c), BF16),
        compiler_params=_params("parallel"),
        name="cast_pad_rows",
    )(w)


def _cast_pad_cols_kernel(x_ref, o_ref, *, n_parts, part, part_padded):
    for p in range(n_parts):
        o_ref[:, p * part_padded:p * part_padded + part] = x_ref[:, p * part:(p + 1) * part].astype(o_ref.dtype)
        if part_padded > part:
            o_ref[:, p * part_padded + part:(p + 1) * part_padded] = jnp.zeros(
                (o_ref.shape[0], part_padded - part), o_ref.dtype)


def _cast_pad_cols(w, layer, n_parts, part_padded, tr):
    _, r, c = w.shape
    part = c // n_parts
    assert part % LANES == 0 and part_padded % LANES == 0 and r % tr == 0
    return pl.pallas_call(
        functools.partial(_cast_pad_cols_kernel, n_parts=n_parts, part=part, part_padded=part_padded),
        grid=(r // tr,),
        in_specs=[pl.BlockSpec((None, tr, c), lambda i: (layer, i, 0))],
        out_specs=pl.BlockSpec((tr, n_parts * part_padded), lambda i: (i, 0)),
        out_shape=jax.ShapeDtypeStruct((r, n_parts * part_padded), BF16),
        compiler_params=_params("parallel"),
        name="cast_pad_cols",
    )(w)


def _gates_kernel(x_ref, w_ref, ba_ref, bb_ref, tri_ref, oa_ref, ob_ref, oc_ref, carry, *, tiles_per_seq, n_ig):
    i = pl.program_id(0)
    tm = x_ref.shape[0]
    z = _dot(x_ref[...], w_ref[...])
    za = z[:, :LANES] + ba_ref[...]
    ls = _log_sigmoid(z[:, LANES:] + bb_ref[...])
    tri = tri_ref[...]
    hi, mid, lo = _split3(ls)
    cum = _dot(tri, hi) + _dot(tri, mid) + _dot(tri, lo)
    if tiles_per_seq > 1:
        @pl.when(i % tiles_per_seq == 0)
        def _():
            carry[...] = jnp.zeros_like(carry)
        cum = cum + carry[0:1, :]
        carry[...] = jnp.broadcast_to(cum[tm - 1:tm, :], carry.shape)
    lane = lax.broadcasted_iota(jnp.int32, (tm, LANES), 1)
    oa_ref[...] = jnp.where(lane < n_ig, za - cum, za)
    ob_ref[...] = cum
    oc_ref[...] = ls


def _gates(x, w_small, bias_a, bias_b, tri, tiles_per_seq, n_ig):
    n, d = x.shape
    tm = tri.shape[0]
    assert n % tm == 0
    out = jax.ShapeDtypeStruct((n, LANES), F32)
    return pl.pallas_call(
        functools.partial(_gates_kernel, tiles_per_seq=tiles_per_seq, n_ig=n_ig),
        grid=(n // tm,),
        in_specs=[pl.BlockSpec((tm, d), lambda i: (i, 0)),
                  pl.BlockSpec((d, 2 * LANES), lambda i: (0, 0)),
                  pl.BlockSpec((1, LANES), lambda i: (0, 0)),
                  pl.BlockSpec((1, LANES), lambda i: (0, 0)),
                  pl.BlockSpec((tm, tm), lambda i: (0, 0))],
        out_specs=[pl.BlockSpec((tm, LANES), lambda i: (i, 0))] * 3,
        out_shape=[out, out, out],
        scratch_shapes=[pltpu.VMEM((SUBLANES, LANES), F32)],
        compiler_params=_params("arbitrary"),
        name="gates",
    )(x, w_small, bias_a, bias_b, tri)


def _mlstm_kernel(q_ref, k_ref, v_ref, o_ref, ga_ref, gb_ref, arow_ref, c0_ref, n0_ref, m0_ref, g_ref,
                  h_ref, c_ref, n_ref, m_ref, ct_sc, n_sc, a_sc, *, n_heads):
    L = q_ref.shape[0]
    dk = HEAD_DIM

    def init():
        ct_sc[...] = c0_ref[...]
        n_sc[...] = jnp.broadcast_to(n0_ref[...], n_sc.shape)
        a_sc[...] = jnp.broadcast_to(m0_ref[...], a_sc.shape)

    def main():
        lane = lax.broadcasted_iota(jnp.int32, (L, LANES), 1)
        ti = lax.broadcasted_iota(jnp.int32, (L, L), 0)
        si = lax.broadcasted_iota(jnp.int32, (L, L), 1)
        causal = si <= ti
        ga = ga_ref[...]
        gb = gb_ref[...]
        for hd in range(n_heads):
            head(hd, lane, causal, ga, gb)

    def head(hd, lane, causal, ga, gb):
        hs = slice(hd * dk, (hd + 1) * dk)
        q = q_ref[:, hs]
        k = k_ref[:, hs]
        v = v_ref[:, hs]
        sel = lane == hd
        a_col = jnp.sum(jnp.where(sel, ga, 0.0), axis=1, keepdims=True)
        b_col = jnp.sum(jnp.where(sel, gb, 0.0), axis=1, keepdims=True)
        a_row = arow_ref[hd]
        a_prev = a_sc[hd, 0:1, 0:1]
        ct = ct_sc[hd]
        n_row = n_sc[hd, 0:1, :]
        mm = jnp.where(causal, a_row, NEG_INF)
        a_t = jnp.maximum(jnp.max(mm, axis=1, keepdims=True), a_prev)
        s = _dot_nt(q, k) * (dk ** -0.5) * jnp.exp(mm - a_t)
        inter = jnp.exp(a_prev - a_t)
        num = _dot(s.astype(BF16), v) + inter * _dot(q, ct.astype(BF16))
        den = (jnp.sum(s, axis=1, keepdims=True)
               + inter * jnp.sum(q.astype(F32) * n_row, axis=1, keepdims=True))
        h = num / jnp.maximum(jnp.abs(den), jnp.exp(-(b_col + a_t)))
        hm = _sigmoid(o_ref[:, hs].astype(F32)) * h
        hm = hm * lax.rsqrt(jnp.mean(hm * hm, axis=1, keepdims=True) + NORM_EPS) * g_ref[:, hs]
        h_ref[:, hs] = hm.astype(h_ref.dtype)

        a_end = jnp.maximum(jnp.max(a_row, axis=1, keepdims=True), a_prev)
        e_col = jnp.exp(a_col - a_end) * (dk ** -0.5)
        decay = jnp.exp(a_prev - a_end)
        ke = k.astype(F32) * e_col
        ct_new = decay * ct + _dot_tn(ke.astype(BF16), v)
        n_new = decay * n_row + jnp.sum(ke, axis=0, keepdims=True)
        ct_sc[hd] = ct_new
        n_sc[hd] = jnp.broadcast_to(n_new, n_sc.shape[1:])
        a_sc[hd] = jnp.broadcast_to(a_end, a_sc.shape[1:])
        c_ref[hd] = ct_new
        n_ref[hd] = n_new
        m_ref[hd] = jnp.broadcast_to(b_col[L - 1:L, :] + a_end, m_ref.shape[1:])

    return init, main


def _mlstm(zm, ga, gb, a_rows, c0t, n0, m0, g_norm, batch, n_heads, seq):
    L = CHUNK
    nc = seq // L
    n = batch * seq
    dh = HEAD_DIM
    hw = n_heads * dh

    def col(off):
        return pl.BlockSpec((L, hw), lambda b, c: (b * nc + c, off))

    def rows():
        return pl.BlockSpec((L, LANES), lambda b, c: (b * nc + c, 0))

    def per_b(r, w):
        return pl.BlockSpec((None, n_heads, r, w), lambda b, c: (b, 0, 0, 0))

    return dict(
        body=functools.partial(_mlstm_kernel, n_heads=n_heads),
        in_specs=[col(0), col(1), col(2), col(3), rows(), rows(),
                  pl.BlockSpec((None, n_heads, 1, L), lambda b, c: (b, 0, 0, c)),
                  per_b(dh, dh), per_b(1, dh), per_b(1, LANES),
                  pl.BlockSpec((1, hw), lambda b, c: (0, 0))],
        out_specs=[pl.BlockSpec((L, hw), lambda b, c: (b * nc + c, 0)),
                   per_b(dh, dh), per_b(1, dh), per_b(1, LANES)],
        out_shape=[jax.ShapeDtypeStruct((n, hw), BF16),
                   jax.ShapeDtypeStruct((batch, n_heads, dh, dh), F32),
                   jax.ShapeDtypeStruct((batch, n_heads, 1, dh), F32),
                   jax.ShapeDtypeStruct((batch, n_heads, 1, LANES), F32)],
        scratch_shapes=[pltpu.VMEM((n_heads, dh, dh), F32), pltpu.VMEM((n_heads, SUBLANES, dh), F32),
                        pltpu.VMEM((n_heads, SUBLANES, LANES), F32)],
        args=(zm, zm, zm, zm, ga, gb, a_rows, c0t, n0, m0, g_norm))


def _recurrent_mixers(parts, batch, seq):
    n_in = [len(p["in_specs"]) for p in parts]
    n_out = [len(p["out_specs"]) for p in parts]
    n_scr = [len(p["scratch_shapes"]) for p in parts]

    def kernel(*refs):
        ins, outs, scrs = refs[:sum(n_in)], refs[sum(n_in):sum(n_in) + sum(n_out)], refs[sum(n_in) + sum(n_out):]
        i = o = s = 0
        stages = []
        for p, ni, no, ns in zip(parts, n_in, n_out, n_scr):
            stages.append(p["body"](*ins[i:i + ni], *outs[o:o + no], *scrs[s:s + ns]))
            i, o, s = i + ni, o + no, s + ns

        @pl.when(pl.program_id(1) == 0)
        def _():
            for init, _ in stages:
                init()

        for _, main in stages:
            main()

    res = pl.pallas_call(
        kernel,
        grid=(batch, seq // CHUNK),
        in_specs=[sp for p in parts for sp in p["in_specs"]],
        out_specs=[sp for p in parts for sp in p["out_specs"]],
        out_shape=[sh for p in parts for sh in p["out_shape"]],
        scratch_shapes=[sc for p in parts for sc in p["scratch_shapes"]],
        compiler_params=_params("parallel", "arbitrary"),
        name="recurrent_mixers",
    )(*[a for p in parts for a in p["args"]])
    out, o = [], 0
    for no in n_out:
        out.append(res[o:o + no])
        o += no
    return out


def _gla_levels(L):
    levels = []
    w = L // 2
    while w >= 1:
        levels.append(w)
        w //= 2
    return levels


def _gla_consts(L):
    t = np.arange(L)
    row, colj = t[:, None], t[None, :]
    mats = [(colj <= row), (colj > row)]
    masks = []
    for w in _gla_levels(L):
        mid = (t // (2 * w)) * 2 * w + w
        right = t >= mid
        mr = right[:, None] & (colj >= mid[:, None]) & (colj <= row)
        ml = (~right)[:, None] & (colj > row) & (colj < mid[:, None])
        mats.append(mr | ml)
        same = (t[:, None] // (2 * w)) == (t[None, :] // (2 * w))
        masks.append(same & right[:, None] & (~right)[None, :])
    masks.append(row == colj)
    m_all = np.concatenate([m.astype(np.float32) for m in mats], axis=0)
    return m_all, np.stack([m.astype(np.float32) for m in masks])


def _gla_kernel(q_ref, k_ref, v_ref, r_ref, ga_ref, wa_ref, ba_ref, mall_ref, mask_ref, s0_ref, g_ref,
                h_ref, s_ref, st_sc, *, n_valid):
    L = q_ref.shape[0]
    dk = LANES // 2
    dv = HEAD_DIM
    n_lev = mask_ref.shape[0] - 1
    n_pair = st_sc.shape[0]

    def init():
        st_sc[...] = s0_ref[...]

    def main():
        ga = ga_ref[...].astype(BF16)
        mall = mall_ref[...]
        lane = lax.broadcasted_iota(jnp.int32, (1, LANES), 1)
        lm = [(lane < dk).astype(F32), (lane >= dk).astype(F32)]
        for p in range(n_pair):
            pair(p, ga, mall, lm)

    def pair(p, ga, mall, lm):
        ps = slice(p * LANES, (p + 1) * LANES)
        q2 = q_ref[:, ps].astype(F32) * (dk ** -0.5)
        k2 = k_ref[:, ps].astype(F32)
        la = _log_sigmoid(_dot(ga, wa_ref[p]) + ba_ref[p]) * (1.0 / GLA_TAU)
        if n_valid < L:
            valid = lax.broadcasted_iota(jnp.int32, (L, LANES), 0) < n_valid
            la = jnp.where(valid, la, 0.0)
            k2 = jnp.where(valid, k2, 0.0)
        la_hi = la.astype(BF16)
        la_mid = (la - la_hi.astype(F32)).astype(BF16)
        e = jnp.exp(_dot(mall, la_hi) + _dot(mall, la_mid))

        q_in = q2 * e[0:L]
        k_end = (k2 * e[L:2 * L]).astype(BF16)
        decay = e[L - 1:L]
        kq = [(q2 * e[(2 + i) * L:(3 + i) * L], (k2 * e[(2 + i) * L:(3 + i) * L]).astype(BF16))
              for i in range(n_lev)]
        k2b = k2.astype(BF16)
        for hh in range(2):
            hs = slice((2 * p + hh) * dv, (2 * p + hh + 1) * dv)
            v = v_ref[:, hs]
            st = st_sc[p, hh]
            o = _dot_nt((q_in * lm[hh]).astype(BF16), st.astype(BF16))
            a = mask_ref[n_lev] * _dot_nt((q2 * lm[hh]).astype(BF16), k2b)
            for i in range(n_lev):
                qh, kh = kq[i]
                a = a + mask_ref[i] * _dot_nt((qh * lm[hh]).astype(BF16), kh)
            o = o + _dot(a.astype(BF16), v)
            st_new = decay * st + _dot_tn(v, k_end)
            st_sc[p, hh] = st_new
            s_ref[p, hh] = st_new
            o = o * lax.rsqrt(jnp.mean(o * o, axis=1, keepdims=True) + NORM_EPS) * g_ref[:, hs]
            r = r_ref[:, hs].astype(F32)
            h_ref[:, hs] = (r * _sigmoid(r) * o).astype(h_ref.dtype)

    return init, main


def _gla(zg, ga, wa, ba, s0t, g_norm, batch, n_heads, seq, n_valid):
    L = CHUNK
    nc = seq // L
    n = batch * seq
    npair = n_heads // 2
    dv = HEAD_DIM
    m_all, masks = _gla_consts(L)
    m_all = jnp.asarray(m_all, BF16)
    masks = jnp.asarray(masks, F32)
    qkw = npair * LANES
    state = pl.BlockSpec((None, npair, 2, dv, LANES), lambda b, c: (b, 0, 0, 0, 0))

    return dict(
        body=functools.partial(_gla_kernel, n_valid=n_valid),
        in_specs=[pl.BlockSpec((L, qkw), lambda b, c: (b * nc + c, 0)),
                  pl.BlockSpec((L, qkw), lambda b, c: (b * nc + c, 1)),
                  pl.BlockSpec((L, 2 * qkw), lambda b, c: (b * nc + c, 1)),
                  pl.BlockSpec((L, 2 * qkw), lambda b, c: (b * nc + c, 2)),
                  pl.BlockSpec((L, LANES), lambda b, c: (b * nc + c, 0)),
                  pl.BlockSpec(wa.shape, lambda b, c: (0, 0, 0)),
                  pl.BlockSpec(ba.shape, lambda b, c: (0, 0, 0)),
                  pl.BlockSpec(m_all.shape, lambda b, c: (0, 0)),
                  pl.BlockSpec(masks.shape, lambda b, c: (0, 0, 0)),
                  state,
                  pl.BlockSpec((1, n_heads * dv), lambda b, c: (0, 0))],
        out_specs=[pl.BlockSpec((L, n_heads * dv), lambda b, c: (b * nc + c, 0)), state],
        out_shape=[jax.ShapeDtypeStruct((n, n_heads * dv), BF16),
                   jax.ShapeDtypeStruct((batch, npair, 2, dv, LANES), F32)],
        scratch_shapes=[pltpu.VMEM((npair, 2, dv, LANES), F32)],
        args=(zg, zg, zg, zg, ga, wa, ba, m_all, masks, s0t, g_norm))


def _fox_prompt_kernel(qi_ref, ki_ref, q_ref, k_ref, v_ref, gb_ref, crow_ref, o_ref, m_sc, acc_sc, ccol_sc,
                       *, n_heads, lane0):
    g = pl.program_id(0)
    p = pl.program_id(1)
    qi = qi_ref[p]
    ki = ki_ref[p]
    tq = q_ref.shape[0]
    tk = k_ref.shape[0]
    dh = HEAD_DIM
    hpg = q_ref.shape[1] // dh
    groups_per_batch = n_heads // hpg

    @pl.when(ki == 0)
    def _():
        m_sc[...] = jnp.full_like(m_sc, NEG_INF)
        acc_sc[...] = jnp.zeros_like(acc_sc)
        lane = lax.broadcasted_iota(jnp.int32, (tq, LANES), 1)
        gb = gb_ref[...]
        for hh in range(hpg):
            hd = (g % groups_per_batch) * hpg + hh
            ccol_sc[hh] = LOG2E * jnp.sum(jnp.where(lane == lane0 + hd, gb, 0.0), axis=1, keepdims=True)

    def step(diagonal):
        for hh in range(hpg):
            hs = slice(hh * dh, (hh + 1) * dh)
            u = _dot_nt(q_ref[:, hs], k_ref[:, hs]) - LOG2E * crow_ref[hh]
            if diagonal:
                ti = lax.broadcasted_iota(jnp.int32, (tq, tk), 0)
                si = lax.broadcasted_iota(jnp.int32, (tq, tk), 1)
                u = jnp.where(si <= ti, u, NEG_INF)
            c2 = ccol_sc[hh]
            m_old = m_sc[hh]
            m_new = jnp.maximum(m_old, jnp.max(u, axis=1, keepdims=True) + c2)
            pr = jnp.exp2(u - (m_new - c2))
            v1 = jnp.concatenate([v_ref[:, hs], jnp.ones((tk, LANES), BF16)], axis=1)
            acc_sc[hh] = jnp.exp2(m_old - m_new) * acc_sc[hh] + _dot(pr.astype(BF16), v1)
            m_sc[hh] = m_new

    @pl.when(ki < qi)
    def _():
        step(False)

    @pl.when(ki == qi)
    def _():
        step(True)
        for hh in range(hpg):
            acc = acc_sc[hh]
            o_ref[:, hh * dh:(hh + 1) * dh] = (acc[:, :dh] / acc[:, dh:]).astype(o_ref.dtype)


def _fox_prompt(zq, zk, zv, gb, c_rows, batch, n_heads, seq, tile, lane0):
    n = batch * seq
    nq = seq // tile
    dh = HEAD_DIM
    hpg = ATTN_HEADS_PER_STEP
    gpb = n_heads // hpg
    pairs = [(a, b) for a in range(nq) for b in range(a + 1)]
    qi = jnp.asarray([a for a, _ in pairs], jnp.int32)
    ki = jnp.asarray([b for _, b in pairs], jnp.int32)
    c_rows = c_rows.reshape(batch * gpb, hpg, 1, seq)

    def qrow(g, p, qi, ki):
        return (g // gpb) * nq + qi[p]

    def krow(g, p, qi, ki):
        return (g // gpb) * nq + ki[p]

    grid_spec = pltpu.PrefetchScalarGridSpec(
        num_scalar_prefetch=2,
        grid=(batch * gpb, len(pairs)),
        in_specs=[pl.BlockSpec((tile, hpg * dh), lambda g, p, qi, ki: (qrow(g, p, qi, ki), g % gpb)),
                  pl.BlockSpec((tile, hpg * dh), lambda g, p, qi, ki: (krow(g, p, qi, ki), g % gpb)),
                  pl.BlockSpec((tile, hpg * dh), lambda g, p, qi, ki: (krow(g, p, qi, ki), g % gpb)),
                  pl.BlockSpec((tile, LANES), lambda g, p, qi, ki: (qrow(g, p, qi, ki), 0)),
                  pl.BlockSpec((None, hpg, 1, tile), lambda g, p, qi, ki: (g, 0, 0, ki[p]))],
        out_specs=pl.BlockSpec((tile, hpg * dh), lambda g, p, qi, ki: (qrow(g, p, qi, ki), g % gpb)),
        scratch_shapes=[pltpu.VMEM((hpg, tile, 1), F32), pltpu.VMEM((hpg, tile, dh + LANES), F32),
                        pltpu.VMEM((hpg, tile, 1), F32)],
    )
    return pl.pallas_call(
        functools.partial(_fox_prompt_kernel, n_heads=n_heads, lane0=lane0),
        grid_spec=grid_spec,
        out_shape=jax.ShapeDtypeStruct((n, n_heads * dh), BF16),
        compiler_params=_params("parallel", "arbitrary"),
        name="fox_prompt",
    )(qi, ki, zq, zk, zv, gb, c_rows)


def _fox_sample_kernel(pt_ref, q_ref, cq_ref, kn_ref, vn_ref, bn_ref, *rest, n_heads, n_new, group):
    k_refs, v_refs, lf_refs = rest[:group], rest[group:2 * group], rest[2 * group:3 * group]
    o_ref, m_sc, l_sc, acc_sc, carry_sc = rest[3 * group:]
    j = pl.program_id(1)
    nr = q_ref.shape[0]
    rows_pp = k_refs[0].shape[0]
    d = q_ref.shape[1]
    n_tiles = rows_pp // LANES
    scale = d ** -0.5

    @pl.when(j == 0)
    def _():
        m_sc[...] = jnp.full_like(m_sc, NEG_INF)
        l_sc[...] = jnp.zeros_like(l_sc)
        acc_sc[...] = jnp.zeros_like(acc_sc)
        carry_sc[...] = jnp.zeros_like(carry_sc)

    lane8 = lax.broadcasted_iota(jnp.int32, (SUBLANES, LANES), 1)
    row8 = lax.broadcasted_iota(jnp.int32, (SUBLANES, LANES), 0)

    def page_suffix(x):
        y = x
        z = x
        sh = n_heads
        while sh < LANES:
            y = y + jnp.where(lane8 + sh < LANES, pltpu.roll(y, LANES - sh, 1), 0.0)
            z = z + pltpu.roll(z, sh, 1)
            sh *= 2
        w = z
        sh = 1
        while sh < SUBLANES:
            w = w + jnp.where(row8 + sh < SUBLANES, pltpu.roll(w, SUBLANES - sh, 0), 0.0)
            sh *= 2
        return y - x + (w - z), jnp.broadcast_to(w[0:1, :], x.shape)

    q = q_ref[...]
    cq = cq_ref[...]
    rowi = lax.broadcasted_iota(jnp.int32, (nr, LANES), 0)
    lanei = lax.broadcasted_iota(jnp.int32, (nr, LANES), 1)
    head_ok = (rowi // n_new) == (lanei % n_heads)

    def update(tile_groups, vbs):
        flat = [t for ts in tile_groups for t in ts]
        m_old = m_sc[...]
        mx = flat[0]
        for t in flat[1:]:
            mx = jnp.maximum(mx, t)
        m_new = jnp.maximum(m_old, jnp.max(mx, axis=1, keepdims=True))
        alpha = jnp.exp(m_old - m_new)
        tot = None
        acc = alpha * acc_sc[...]
        for ts, vb in zip(tile_groups, vbs):
            ps = [jnp.exp(t - m_new) for t in ts]
            for t in ps:
                tot = t if tot is None else tot + t
            pcat = ps[0] if len(ps) == 1 else jnp.concatenate(ps, axis=1)
            acc = acc + _dot(pcat.astype(BF16), vb)
        l_sc[...] = alpha * l_sc[...] + jnp.sum(tot, axis=1, keepdims=True)
        acc_sc[...] = acc
        m_sc[...] = m_new

    carry = carry_sc[...]
    tile_groups = []
    for g in range(group):
        within, total = page_suffix(lf_refs[g][...])
        suffix = within + carry
        carry = carry + total
        s = _dot_nt(q, k_refs[g][...].astype(BF16)) * scale
        tile_groups.append([jnp.where(head_ok, s[:, r * LANES:(r + 1) * LANES] + cq + suffix[r:r + 1, :], NEG_INF)
                            for r in range(n_tiles)])
    carry_sc[...] = carry
    update(tile_groups, [v_refs[g][...].astype(BF16) for g in range(group)])

    @pl.when(j == pl.num_programs(1) - 1)
    def _():
        s_self = _dot_nt(q, kn_ref[...]) * scale + cq + bn_ref[...]
        ok = head_ok & (lanei < n_new * n_heads) & ((lanei // n_heads) <= (rowi % n_new))
        update([[jnp.where(ok, s_self, NEG_INF)]], [vn_ref[...]])
        o_ref[...] = acc_sc[...] / l_sc[...]


def _fox_sample(page_table, q, cq, kn, vn, bn, cache_k, cache_v, cache_lf, layer, n_heads, n_new):
    nb, n_pages = page_table.shape
    nr = q.shape[1]
    rows_pp = cache_k.shape[2]
    d = q.shape[2]
    group = max(g for g in (1, 2, 4, 8) if n_pages % g == 0)

    def page_spec(rows, width, g):
        return pl.BlockSpec((None, None, rows, width),
                            lambda b, j, pt: (layer, pt[b, n_pages - 1 - (j * group + g)], 0, 0))

    per_b = lambda rows, width: pl.BlockSpec((None, rows, width), lambda b, j, pt: (b, 0, 0))
    grid_spec = pltpu.PrefetchScalarGridSpec(
        num_scalar_prefetch=1,
        grid=(nb, n_pages // group),
        in_specs=([per_b(nr, d), per_b(nr, LANES), per_b(LANES, d), per_b(LANES, d), per_b(1, LANES)]
                  + [page_spec(rows_pp, d, g) for g in range(group)]
                  + [page_spec(rows_pp, d, g) for g in range(group)]
                  + [page_spec(SUBLANES, LANES, g) for g in range(group)]),
        out_specs=per_b(nr, d),
        scratch_shapes=[pltpu.VMEM((nr, 1), F32), pltpu.VMEM((nr, 1), F32), pltpu.VMEM((nr, d), F32),
                        pltpu.VMEM((SUBLANES, LANES), F32)],
    )
    return pl.pallas_call(
        functools.partial(_fox_sample_kernel, n_heads=n_heads, n_new=n_new, group=group),
        grid_spec=grid_spec,
        out_shape=jax.ShapeDtypeStruct((nb, nr, d), F32),
        compiler_params=_params("parallel", "arbitrary"),
        name="fox_sample",
    )(page_table, q, cq, kn, vn, bn, *([cache_k] * group), *([cache_v] * group), *([cache_lf] * group))


def _layer_norm(r, g, b):
    mu = jnp.mean(r, axis=1, keepdims=True)
    xc = r - mu
    var = jnp.mean(xc * xc, axis=1, keepdims=True)
    return xc * lax.rsqrt(var + LN_EPS) * g + b


def _outproj_kernel(hm_ref, hg_ref, hf_ref, wm_ref, wg_ref, wf_ref, x_ref, g_ref, b_ref, h_ref, hb_ref, *, alpha):
    mix = _dot(hm_ref[...], wm_ref[...]) + _dot(hg_ref[...], wg_ref[...]) + _dot(hf_ref[...], wf_ref[...])
    h = _layer_norm(alpha * x_ref[...] + mix, g_ref[...], b_ref[...])
    h_ref[...] = h
    hb_ref[...] = h.astype(BF16)


def _outproj_ln(hm, hg, hf, w_out, x, g, b, alpha, tm):
    n, d = x.shape
    tm = min(tm, n)
    wm, wg, wf = hm.shape[1], hg.shape[1], hf.shape[1]
    assert wm == wg and wf == wm + wg and n % tm == 0
    const = pl.Buffered(1)
    return pl.pallas_call(
        functools.partial(_outproj_kernel, alpha=alpha),
        grid=(n // tm,),
        in_specs=[pl.BlockSpec((tm, wm), lambda i: (i, 0)),
                  pl.BlockSpec((tm, wg), lambda i: (i, 0)),
                  pl.BlockSpec((tm, wf), lambda i: (i, 0)),
                  pl.BlockSpec((wm, d), lambda i: (0, 0), pipeline_mode=const),
                  pl.BlockSpec((wg, d), lambda i: (1, 0), pipeline_mode=const),
                  pl.BlockSpec((wf, d), lambda i: (1, 0), pipeline_mode=const),
                  pl.BlockSpec((tm, d), lambda i: (i, 0)),
                  pl.BlockSpec((1, d), lambda i: (0, 0)),
                  pl.BlockSpec((1, d), lambda i: (0, 0))],
        out_specs=[pl.BlockSpec((tm, d), lambda i: (i, 0))] * 2,
        out_shape=[jax.ShapeDtypeStruct((n, d), F32), jax.ShapeDtypeStruct((n, d), BF16)],
        compiler_params=_params("parallel"),
        name="outproj_ln",
    )(hm, hg, hf, w_out, w_out, w_out, x, g, b)


def _ffn_up_kernel(x_ref, wg_ref, wv_ref, cwg_ref, cwv_ref, cbg_ref, cbv_ref, hg0_ref, hv0_ref,
                   o_ref, tg_ref, tv_ref, halo_g, halo_v, win_g, win_v, *, tiles_per_seq, shift):
    i = pl.program_id(0)
    j = pl.program_id(1)
    tm = x_ref.shape[0]
    hr = hg0_ref.shape[0]
    tn = o_ref.shape[1]

    @pl.when(i % tiles_per_seq == 0)
    def _():
        halo_g[j] = hg0_ref[...]
        halo_v[j] = hv0_ref[...]

    rc = min(FFN_ROW_CHUNK, tm)

    rb = min(FFN_EPILOGUE_ROWS, rc)

    def conv(r0, cs, cw_ref, cb_ref, win):
        cw = cw_ref[:, cs]
        return (cb_ref[:, cs] + cw[0:1] * win[pl.ds(hr + r0 - 2 * shift, rb), cs]
                + cw[1:2] * win[pl.ds(hr + r0 - shift, rb), cs] + cw[2:3] * win[pl.ds(hr + r0, rb), cs])

    win_g[0:hr, :] = halo_g[j]
    win_v[0:hr, :] = halo_v[j]
    for c0 in range(0, tn, FFN_COL_SLAB):
        cs = slice(c0, c0 + FFN_COL_SLAB)
        for r0 in range(0, tm, rc):
            rs = slice(r0, r0 + rc)
            ug = _dot(x_ref[rs, :], wg_ref[:, cs])
            uv = _dot(x_ref[rs, :], wv_ref[:, cs])
            win_g[hr + r0:hr + r0 + rc, cs] = ug
            win_v[hr + r0:hr + r0 + rc, cs] = uv
            for r1 in range(r0, r0 + rc, rb):
                yg = conv(r1, cs, cwg_ref, cbg_ref, win_g)
                yv = conv(r1, cs, cwv_ref, cbv_ref, win_v)
                o_ref[r1:r1 + rb, cs] = (yg * _sigmoid(yg) * yv).astype(o_ref.dtype)
    tail_g = win_g[tm:tm + hr, :]
    tail_v = win_v[tm:tm + hr, :]
    tg_ref[...] = tail_g
    tv_ref[...] = tail_v
    halo_g[j] = tail_g
    halo_v[j] = tail_v


def _ffn_up(x, w_up, conv_w, conv_b, halo0, fp, tm, tn, tiles_per_seq, shift):
    n, d = x.shape
    tm = min(tm, n)
    nj = fp // tn
    ni = n // tm
    hr = halo0.shape[0] // (ni // tiles_per_seq)
    assert n % tm == 0 and fp % tn == 0 and hr >= 2 * shift
    seq = lambda i: i // tiles_per_seq
    return pl.pallas_call(
        functools.partial(_ffn_up_kernel, tiles_per_seq=tiles_per_seq, shift=shift),
        grid=(ni, nj),
        in_specs=[pl.BlockSpec((tm, d), lambda i, j: (i, 0)),
                  pl.BlockSpec((d, tn), lambda i, j: (0, j)),
                  pl.BlockSpec((d, tn), lambda i, j: (0, nj + j)),
                  pl.BlockSpec((CONV_W, tn), lambda i, j: (0, j)),
                  pl.BlockSpec((CONV_W, tn), lambda i, j: (0, nj + j)),
                  pl.BlockSpec((1, tn), lambda i, j: (0, j)),
                  pl.BlockSpec((1, tn), lambda i, j: (0, nj + j)),
                  pl.BlockSpec((hr, tn), lambda i, j: (seq(i), j)),
                  pl.BlockSpec((hr, tn), lambda i, j: (seq(i), nj + j))],
        out_specs=[pl.BlockSpec((tm, tn), lambda i, j: (i, j)),
                   pl.BlockSpec((hr, tn), lambda i, j: (i, j)),
                   pl.BlockSpec((hr, tn), lambda i, j: (i, j))],
        out_shape=[jax.ShapeDtypeStruct((n, fp), BF16),
                   jax.ShapeDtypeStruct((ni * hr, fp), F32),
                   jax.ShapeDtypeStruct((ni * hr, fp), F32)],
        scratch_shapes=[pltpu.VMEM((nj, hr, tn), F32), pltpu.VMEM((nj, hr, tn), F32),
                        pltpu.VMEM((hr + tm, tn), F32), pltpu.VMEM((hr + tm, tn), F32)],
        compiler_params=_params("arbitrary", "arbitrary"),
        name="ffn_up",
    )(x, w_up, w_up, conv_w, conv_w, conv_b, conv_b, halo0, halo0)


def _ffn_down_kernel(a_ref, w_ref, h_ref, g_ref, b_ref, x_ref, xb_ref, *, alpha):
    x = _layer_norm(alpha * h_ref[...] + _dot(a_ref[...], w_ref[...]), g_ref[...], b_ref[...])
    x_ref[...] = x
    xb_ref[...] = x.astype(BF16)


def _ffn_down_ln(a, w_down, h, g, b, alpha, tm):
    n, fp = a.shape
    d = w_down.shape[1]
    tm = min(tm, n)
    assert n % tm == 0
    return pl.pallas_call(
        functools.partial(_ffn_down_kernel, alpha=alpha),
        grid=(n // tm,),
        in_specs=[pl.BlockSpec((tm, fp), lambda i: (i, 0)),
                  pl.BlockSpec((fp, d), lambda i: (0, 0), pipeline_mode=pl.Buffered(1)),
                  pl.BlockSpec((tm, d), lambda i: (i, 0)),
                  pl.BlockSpec((1, d), lambda i: (0, 0)),
                  pl.BlockSpec((1, d), lambda i: (0, 0))],
        out_specs=[pl.BlockSpec((tm, d), lambda i: (i, 0))] * 2,
        out_shape=[jax.ShapeDtypeStruct((n, d), F32), jax.ShapeDtypeStruct((n, d), BF16)],
        compiler_params=_params("parallel"),
        name="ffn_down_ln",
    )(a, w_down, h, g, b)


def _prep_layer(l, dims, w_in, b_m_ig, b_m_fg, w_g_alpha_up, b_g_alpha, b_f, g_m_norm, g_g_norm, w_out,
                ln1_g, ln1_b, w_up, conv_w, conv_b, w_down, ln2_g, ln2_b):
    d, hm, hg, hf, f, fp = dims
    dh = HEAD_DIM
    dkg = dh // 2
    sizes = [hm * dh] * 4 + [hm, hm] + [hg * dkg] * 2 + [hg * dh] * 2 + [GLA_RANK] + [hf * dh] * 3 + [hf]
    offs = np.concatenate([[0], np.cumsum(sizes)]).tolist()
    (o_mq, o_mk, o_mv, o_mo, o_mi, o_mf, o_gq, o_gk, o_gv, o_gr, o_ga, o_fq, o_fk, o_fv, o_ff, _) = offs
    w = w_in[l]
    col = lambda a, b: w[:, a:b]
    zeros = lambda c: jnp.zeros((d, c), F32)
    blk_a = jnp.concatenate([col(o_mi, o_mi + hm), zeros(8 - hm), col(o_ga, o_ga + GLA_RANK),
                             zeros(LANES - 8 - GLA_RANK)], axis=1)
    blk_b = jnp.concatenate([col(o_mf, o_mf + hm), zeros(8 - hm), col(o_ff, o_ff + hf),
                             zeros(LANES - 8 - hf)], axis=1)
    pad1 = lambda v, lo, total: jnp.pad(v, (lo, total - lo - v.shape[0]))[None, :]
    wa = jnp.zeros((LANES, hg * dkg), F32).at[8:8 + GLA_RANK].set(w_g_alpha_up[l])
    npair = hg // 2
    pad_f = lambda a: jnp.pad(a, [(0, 0)] * (a.ndim - 1) + [(0, fp - f)])
    return dict(
        w_m=col(o_mq, o_mi).astype(BF16),
        w_g=col(o_gq, o_ga).astype(BF16),
        w_fq=col(o_fq, o_fk).astype(BF16),
        w_fk=col(o_fk, o_fv).astype(BF16),
        w_fv=col(o_fv, o_ff).astype(BF16),
        w_small=jnp.concatenate([blk_a, blk_b], axis=1).astype(BF16),
        bias_a=pad1(b_m_ig[l], 0, LANES),
        bias_b=pad1(b_m_fg[l], 0, LANES) + pad1(b_f[l], 8, LANES),
        wa=wa.reshape(LANES, npair, LANES).transpose(1, 0, 2).astype(BF16),
        ba=b_g_alpha[l].reshape(npair, 1, LANES),
        g_m=g_m_norm[l][None, :], g_g=g_g_norm[l][None, :],
        w_out=w_out[l].astype(BF16),
        ln1_g=ln1_g[l][None, :], ln1_b=ln1_b[l][None, :], ln2_g=ln2_g[l][None, :], ln2_b=ln2_b[l][None, :],
        w_up=_cast_pad_cols(w_up, l, 2, fp, 256),
        conv_w=jnp.concatenate([pad_f(conv_w[l][:, :f]), pad_f(conv_w[l][:, f:])], axis=1),
        conv_b=jnp.concatenate([pad_f(conv_b[l][:f]), pad_f(conv_b[l][f:])])[None, :],
        w_down=_cast_pad_rows(w_down, l, fp, 512),
    )


def _pad_conv_state(s, f, fp):
    pad = [(0, 0)] * (s.ndim - 1) + [(0, fp - f)]
    return jnp.concatenate([jnp.pad(s[..., :f], pad), jnp.pad(s[..., f:], pad)], axis=-1)


def _mixer_rows(x_bf, p, tm, q_scale=1.0):
    zm, = _matmul(x_bf, p["w_m"], [BF16], tm, 512, "proj_m")
    zg, = _matmul(x_bf, p["w_g"], [BF16], tm, 512, "proj_g")
    zq, = _matmul(x_bf, p["w_fq"], [BF16], tm, 512, "proj_fq", scale=q_scale)
    zk, zkb = _matmul(x_bf, p["w_fk"], [F32, BF16], tm, 512, "proj_fk")
    zv, zvb = _matmul(x_bf, p["w_fv"], [F32, BF16], tm, 512, "proj_fv")
    return zm, zg, zq, zk, zkb, zv, zvb


def kernel(x_prompt, x_sample, state_mlstm_c, state_mlstm_n, state_mlstm_m, state_gla, state_ffn_conv,
           cache_k, cache_v, cache_logf, page_table,
           w_in, b_m_ig, b_m_fg, w_g_alpha_up, b_g_alpha, b_f, g_m_norm, g_g_norm, w_out,
           ln1_g, ln1_b, w_up, conv_w, conv_b, w_down, ln2_g, ln2_b):
    bp, seq, d = x_prompt.shape
    db, ns, _ = x_sample.shape
    depth = w_in.shape[0]
    hm = b_m_ig.shape[1]
    hf = b_f.shape[1]
    dh = HEAD_DIM
    hg = g_g_norm.shape[1] // dh
    dkg = dh // 2
    f = w_down.shape[1]
    fp = -(-f // 512) * 512
    alpha = (2.0 * depth) ** 0.25
    n_pool, page = cache_k.shape[1], cache_k.shape[2]
    assert hm <= 8 and hf <= 8 and hg % 2 == 0 and seq % 512 == 0 and ns <= CHUNK
    assert page * hf == SUBLANES * LANES and ns * hf <= LANES
    dims = (d, hm, hg, hf, f, fp)
    L = CHUNK
    n_p = bp * seq

    tri_p = jnp.asarray(np.tril(np.ones((GATE_TILE, GATE_TILE), np.float32)), BF16)
    gs_rows = LANES
    r = np.arange(gs_rows)
    tri_s_np = ((r[:, None] % db == r[None, :] % db) & (r[None, :] // db <= r[:, None] // db)
                & (r[:, None] < ns * db) & (r[None, :] < ns * db))
    tri_s = jnp.asarray(tri_s_np.astype(np.float32), BF16)

    xp = x_prompt.astype(F32).reshape(n_p, d)
    xp_bf = xp.astype(BF16)
    xs = x_sample.astype(F32).transpose(1, 0, 2).reshape(ns * db, d)
    xs_bf = xs.astype(BF16)

    ck = cache_k.astype(F32).reshape(depth, n_pool, page * hf, dh)
    cv = cache_v.astype(F32).reshape(depth, n_pool, page * hf, dh)
    clf = cache_logf.astype(F32).reshape(depth, n_pool, SUBLANES, LANES)

    def to_padded(z, pad_value=0.0, mode="constant"):
        c = z.shape[-1]
        a = z.reshape(ns, db, c).transpose(1, 0, 2)
        if mode == "edge":
            a = jnp.pad(a, ((0, 0), (0, L - ns), (0, 0)), mode="edge")
        else:
            a = jnp.pad(a, ((0, 0), (0, L - ns), (0, 0)), constant_values=pad_value)
        return a.reshape(db * L, c)

    def from_padded(y):
        c = y.shape[-1]
        return y.reshape(db, L, c)[:, :ns].transpose(1, 0, 2).reshape(ns * db, c)

    def head_rows(g, lane0, nh, batch, t):
        return g[:, lane0:lane0 + nh].reshape(batch, t, nh).transpose(0, 2, 1).reshape(batch * nh, 1, t)

    outs_p = [[] for _ in range(8)]
    outs_s = [[] for _ in range(8)]
    for l in range(depth):
        p = _prep_layer(l, dims, w_in, b_m_ig, b_m_fg, w_g_alpha_up, b_g_alpha, b_f, g_m_norm, g_g_norm, w_out,
                        ln1_g, ln1_b, w_up, conv_w, conv_b, w_down, ln2_g, ln2_b)

        zm, zg, zq, zk, zkb, zv, zvb = _mixer_rows(xp_bf, p, 1024, q_scale=LOG2E * dh ** -0.5)
        ga, gb, gc = _gates(xp_bf, p["w_small"], p["bias_a"], p["bias_b"], tri_p, seq // GATE_TILE, hm)
        zeros = lambda *s: jnp.zeros(s, F32)
        (h_g, st), (h_m, ct, n_m, m_m) = _recurrent_mixers(
            [_gla(zg, ga, p["wa"], p["ba"], zeros(bp, hg // 2, 2, dh, LANES), p["g_g"], bp, hg, seq, L),
             _mlstm(zm, ga, gb, head_rows(ga, 0, hm, bp, seq).reshape(bp, hm, 1, seq),
                    zeros(bp, hm, dh, dh), zeros(bp, hm, 1, dh), zeros(bp, hm, 1, LANES), p["g_m"], bp, hm, seq)],
            bp, seq)
        h_f = _fox_prompt(zq, zkb, zvb, gb, head_rows(gb, 8, hf, bp, seq), bp, hf, seq, ATTN_TILE, 8)
        h, h_bf = _outproj_ln(h_m, h_g, h_f, p["w_out"], xp, p["ln1_g"], p["ln1_b"], alpha, 512)
        tm_f = min(FFN_ROW_TILE, seq)
        act, tg, tv = _ffn_up(h_bf, p["w_up"], p["conv_w"], p["conv_b"], zeros(bp * SUBLANES, 2 * fp),
                              fp, tm_f, 512, seq // tm_f, 1)
        xp, xp_bf = _ffn_down_ln(act, p["w_down"], h, p["ln2_g"], p["ln2_b"], alpha, FFN_DOWN_ROW_TILE)

        def conv_tail_p(t):
            t = t.reshape(bp, seq // tm_f, SUBLANES, fp)[:, -1, SUBLANES - (CONV_W - 1):, :f]
            return t
        outs_p[0].append(ct.reshape(bp, hm, dh, dh).swapaxes(-1, -2))
        outs_p[1].append(n_m.reshape(bp, hm, dh))
        outs_p[2].append(m_m[:, :, 0, 0])
        st_h = st.reshape(bp, hg // 2, 2, dh, 2, dkg)
        st_h = jnp.stack([st_h[:, :, 0, :, 0, :], st_h[:, :, 1, :, 1, :]], axis=2)
        outs_p[3].append(st_h.reshape(bp, hg, dh, dkg).swapaxes(-1, -2))
        outs_p[4].append(jnp.concatenate([conv_tail_p(tg), conv_tail_p(tv)], axis=-1))
        outs_p[5].append(zk.reshape(bp, seq, hf, dh))
        outs_p[6].append(zv.reshape(bp, seq, hf, dh))
        outs_p[7].append(gc[:, 8:8 + hf].reshape(bp, seq, hf))

        n_s = ns * db
        zm, zg, zq, zk, zkb, zv, zvb = _mixer_rows(xs_bf, p, n_s)
        xs_pad = jnp.pad(xs_bf, ((0, gs_rows - n_s), (0, 0)))
        ga, gb, gc = _gates(xs_pad, p["w_small"], p["bias_a"], p["bias_b"], tri_s, 1, hm)
        ga, gb, gc = ga[:n_s], gb[:n_s], gc[:n_s]
        ga_p = to_padded(ga, NEG_INF)
        gb_p = to_padded(gb, mode="edge")
        c0t = state_mlstm_c[l].astype(F32).swapaxes(-1, -2)
        n0 = state_mlstm_n[l].astype(F32).reshape(db, hm, 1, dh)
        m0 = jnp.broadcast_to(state_mlstm_m[l].astype(F32).reshape(db, hm, 1, 1), (db, hm, 1, LANES))
        s0 = state_gla[l].astype(F32).swapaxes(-1, -2).reshape(db, hg // 2, 2, dh, dkg)
        s0t = jnp.stack([jnp.pad(s0[:, :, 0], ((0, 0), (0, 0), (0, 0), (0, dkg))),
                         jnp.pad(s0[:, :, 1], ((0, 0), (0, 0), (0, 0), (dkg, 0)))], axis=2)
        (h_m, ct, n_m, m_m), (h_g, st) = _recurrent_mixers(
            [_mlstm(to_padded(zm), ga_p, gb_p, head_rows(ga_p, 0, hm, db, L).reshape(db, hm, 1, L),
                    c0t, n0, m0, p["g_m"], db, hm, L),
             _gla(to_padded(zg), to_padded(ga), p["wa"], p["ba"], s0t, p["g_g"], db, hg, L, ns)],
            db, L)

        q_s = zq.reshape(ns, db, hf, dh).transpose(1, 2, 0, 3).reshape(db, hf * ns, dh)
        new_rows = lambda z: jnp.pad(z.reshape(ns, db, hf * dh).transpose(1, 0, 2).reshape(db, ns * hf, dh),
                                     ((0, 0), (0, LANES - ns * hf), (0, 0)))
        c_new = gb[:, 8:8 + hf].reshape(ns, db, hf)
        cq = jnp.broadcast_to(c_new.transpose(1, 2, 0).reshape(db, hf * ns, 1), (db, hf * ns, LANES))
        bn = jnp.pad(-c_new.transpose(1, 0, 2).reshape(db, 1, ns * hf), ((0, 0), (0, 0), (0, LANES - ns * hf)))
        o_f = _fox_sample(page_table, q_s, cq, new_rows(zkb), new_rows(zvb), bn, ck, cv, clf, l, hf, ns)
        h_f = o_f.reshape(db, hf, ns, dh).transpose(2, 0, 1, 3).reshape(n_s, hf * dh).astype(BF16)

        h, h_bf = _outproj_ln(from_padded(h_m), from_padded(h_g), h_f, p["w_out"], xs, p["ln1_g"], p["ln1_b"],
                              alpha, n_s)
        halo_s = _pad_conv_state(state_ffn_conv[l].astype(F32).transpose(1, 0, 2).reshape((CONV_W - 1) * db, 2 * f),
                                 f, fp)
        act, tg, tv = _ffn_up(h_bf, p["w_up"], p["conv_w"], p["conv_b"], halo_s, fp, n_s, 512, 1, db)
        xs, xs_bf = _ffn_down_ln(act, p["w_down"], h, p["ln2_g"], p["ln2_b"], alpha, n_s)

        conv_tail_s = lambda t: t[:, :f].reshape(CONV_W - 1, db, f).transpose(1, 0, 2)
        outs_s[0].append(ct.reshape(db, hm, dh, dh).swapaxes(-1, -2))
        outs_s[1].append(n_m.reshape(db, hm, dh))
        outs_s[2].append(m_m[:, :, 0, 0])
        st_h = st.reshape(db, hg // 2, 2, dh, 2, dkg)
        st_h = jnp.stack([st_h[:, :, 0, :, 0, :], st_h[:, :, 1, :, 1, :]], axis=2)
        outs_s[3].append(st_h.reshape(db, hg, dh, dkg).swapaxes(-1, -2))
        outs_s[4].append(jnp.concatenate([conv_tail_s(tg), conv_tail_s(tv)], axis=-1))
        outs_s[5].append(zk.reshape(ns, db, hf, dh).transpose(1, 0, 2, 3))
        outs_s[6].append(zv.reshape(ns, db, hf, dh).transpose(1, 0, 2, 3))
        outs_s[7].append(gc[:, 8:8 + hf].reshape(ns, db, hf).transpose(1, 0, 2))

    y_p = xp.reshape(bp, seq, d)
    y_s = xs.reshape(ns, db, d).transpose(1, 0, 2)
    return (y_p, y_s) + tuple(jnp.stack(a) for a in outs_p) + tuple(jnp.stack(a) for a in outs_s)
```

```python
import functools

import numpy as np
import jax
import jax.numpy as jnp
from jax import lax
from jax.experimental import pallas as pl
from jax.experimental.pallas import tpu as pltpu

F32 = jnp.float32
BF16 = jnp.bfloat16

HEAD_DIM = 128
GLA_RANK = 16
GLA_TAU = 16.0
CONV_W = 3
LN_EPS = 1e-5
NORM_EPS = 1e-6

LANES = 128
SUBLANES = 8
VMEM_LIMIT = 56 * 1024 * 1024

CHUNK = 128
GATE_TILE = 256
ATTN_TILE = 512
ATTN_HEADS_PER_STEP = 8
FFN_ROW_TILE = 2048
FFN_COL_SLAB = 256
FFN_ROW_CHUNK = 512
FFN_EPILOGUE_ROWS = 512
FFN_DOWN_ROW_TILE = 256
NEG_INF = float("-inf")
LOG2E = 1.4426950408889634


def _params(*sem, flags=None):
    return pltpu.CompilerParams(dimension_semantics=sem, vmem_limit_bytes=VMEM_LIMIT, flags=flags)


def _dot(a, b):
    return jnp.dot(a, b, preferred_element_type=F32)


def _dot_nt(a, b):
    return lax.dot_general(a, b, (((1,), (1,)), ((), ())), preferred_element_type=F32)


def _dot_tn(a, b):
    return lax.dot_general(a, b, (((0,), (0,)), ((), ())), preferred_element_type=F32)


def _log_sigmoid(x):
    return jnp.minimum(x, 0.0) - jnp.log1p(jnp.exp(-jnp.abs(x)))


def _sigmoid(x):
    return 1.0 / (1.0 + jnp.exp(-x))


def _split3(x):
    hi = x.astype(BF16)
    r = x - hi.astype(F32)
    mid = r.astype(BF16)
    lo = (r - mid.astype(F32)).astype(BF16)
    return hi, mid, lo


def _mm_kernel(x_ref, w_ref, *o_refs, scale):
    acc = _dot(x_ref[...], w_ref[...])
    if scale != 1.0:
        acc = acc * scale
    for o in o_refs:
        o[...] = acc.astype(o.dtype)


def _matmul(x, w, out_dtypes, tm, tn, name, scale=1.0):
    m, k = x.shape
    n = w.shape[1]
    tm = min(tm, m)
    tn = min(tn, n)
    assert m % tm == 0 and n % tn == 0
    return pl.pallas_call(
        functools.partial(_mm_kernel, scale=scale),
        grid=(m // tm, n // tn),
        in_specs=[pl.BlockSpec((tm, k), lambda i, j: (i, 0)),
                  pl.BlockSpec((k, tn), lambda i, j: (0, j))],
        out_specs=[pl.BlockSpec((tm, tn), lambda i, j: (i, j)) for _ in out_dtypes],
        out_shape=[jax.ShapeDtypeStruct((m, n), d) for d in out_dtypes],
        compiler_params=_params("parallel", "parallel"),
        name=name,
    )(x, w)


def _cast_pad_rows_kernel(x_ref, o_ref):
    r = x_ref.shape[0]
    o_ref[0:r, :] = x_ref[...].astype(o_ref.dtype)
    if o_ref.shape[0] > r:
        o_ref[r:, :] = jnp.zeros((o_ref.shape[0] - r, o_ref.shape[1]), o_ref.dtype)


def _cast_pad_rows(w, layer, rows_padded, tc):
    _, r, c = w.shape
    assert r % 16 == 0 and rows_padded % 16 == 0 and c % tc == 0
    return pl.pallas_call(
        _cast_pad_rows_kernel,
        grid=(c // tc,),
        in_specs=[pl.BlockSpec((None, r, tc), lambda j: (layer, 0, j))],
        out_specs=pl.BlockSpec((rows_padded, tc), lambda j: (0, j)),
        out_shape=jax.ShapeDtypeStruct((rows_padded, c), BF16),
        compiler_params=_params("parallel"),
        name="cast_pad_rows",
    )(w)


def _cast_pad_cols_kernel(x_ref, o_ref, *, n_parts, part, part_padded):
    for p in range(n_parts):
        o_ref[:, p * part_padded:p * part_padded + part] = x_ref[:, p * part:(p + 1) * part].astype(o_ref.dtype)
        if part_padded > part:
            o_ref[:, p * part_padded + part:(p + 1) * part_padded] = jnp.zeros(
                (o_ref.shape[0], part_padded - part), o_ref.dtype)


def _cast_pad_cols(w, layer, n_parts, part_padded, tr):
    _, r, c = w.shape
    part = c // n_parts
    assert part % LANES == 0 and part_padded % LANES == 0 and r % tr == 0
    return pl.pallas_call(
        functools.partial(_cast_pad_cols_kernel, n_parts=n_parts, part=part, part_padded=part_padded),
        grid=(r // tr,),
        in_specs=[pl.BlockSpec((None, tr, c), lambda i: (layer, i, 0))],
        out_specs=pl.BlockSpec((tr, n_parts * part_padded), lambda i: (i, 0)),
        out_shape=jax.ShapeDtypeStruct((r, n_parts * part_padded), BF16),
        compiler_params=_params("parallel"),
        name="cast_pad_cols",
    )(w)


def _gates_kernel(x_ref, w_ref, ba_ref, bb_ref, tri_ref, oa_ref, ob_ref, oc_ref, carry, *, tiles_per_seq, n_ig):
    i = pl.program_id(0)
    tm = x_ref.shape[0]
    z = _dot(x_ref[...], w_ref[...])
    za = z[:, :LANES] + ba_ref[...]
    ls = _log_sigmoid(z[:, LANES:] + bb_ref[...])
    tri = tri_ref[...]
    hi, mid, lo = _split3(ls)
    cum = _dot(tri, hi) + _dot(tri, mid) + _dot(tri, lo)
    if tiles_per_seq > 1:
        @pl.when(i % tiles_per_seq == 0)
        def _():
            carry[...] = jnp.zeros_like(carry)
        cum = cum + carry[0:1, :]
        carry[...] = jnp.broadcast_to(cum[tm - 1:tm, :], carry.shape)
    lane = lax.broadcasted_iota(jnp.int32, (tm, LANES), 1)
    oa_ref[...] = jnp.where(lane < n_ig, za - cum, za)
    ob_ref[...] = cum
    oc_ref[...] = ls


def _gates(x, w_small, bias_a, bias_b, tri, tiles_per_seq, n_ig):
    n, d = x.shape
    tm = tri.shape[0]
    assert n % tm == 0
    out = jax.ShapeDtypeStruct((n, LANES), F32)
    return pl.pallas_call(
        functools.partial(_gates_kernel, tiles_per_seq=tiles_per_seq, n_ig=n_ig),
        grid=(n // tm,),
        in_specs=[pl.BlockSpec((tm, d), lambda i: (i, 0)),
                  pl.BlockSpec((d, 2 * LANES), lambda i: (0, 0)),
                  pl.BlockSpec((1, LANES), lambda i: (0, 0)),
                  pl.BlockSpec((1, LANES), lambda i: (0, 0)),
                  pl.BlockSpec((tm, tm), lambda i: (0, 0))],
        out_specs=[pl.BlockSpec((tm, LANES), lambda i: (i, 0))] * 3,
        out_shape=[out, out, out],
        scratch_shapes=[pltpu.VMEM((SUBLANES, LANES), F32)],
        compiler_params=_params("arbitrary"),
        name="gates",
    )(x, w_small, bias_a, bias_b, tri)


def _mlstm_kernel(q_ref, k_ref, v_ref, o_ref, ga_ref, gb_ref, arow_ref, c0_ref, n0_ref, m0_ref, g_ref,
                  h_ref, c_ref, n_ref, m_ref, ct_sc, n_sc, a_sc, *, n_heads):
    L = q_ref.shape[0]
    dk = HEAD_DIM

    def init():
        ct_sc[...] = c0_ref[...]
        n_sc[...] = jnp.broadcast_to(n0_ref[...], n_sc.shape)
        a_sc[...] = jnp.broadcast_to(m0_ref[...], a_sc.shape)

    def main():
        lane = lax.broadcasted_iota(jnp.int32, (L, LANES), 1)
        ti = lax.broadcasted_iota(jnp.int32, (L, L), 0)
        si = lax.broadcasted_iota(jnp.int32, (L, L), 1)
        causal = si <= ti
        ga = ga_ref[...]
        gb = gb_ref[...]
        for hd in range(n_heads):
            head(hd, lane, causal, ga, gb)

    def head(hd, lane, causal, ga, gb):
        hs = slice(hd * dk, (hd + 1) * dk)
        q = q_ref[:, hs]
        k = k_ref[:, hs]
        v = v_ref[:, hs]
        sel = lane == hd
        a_col = jnp.sum(jnp.where(sel, ga, 0.0), axis=1, keepdims=True)
        b_col = jnp.sum(jnp.where(sel, gb, 0.0), axis=1, keepdims=True)
        a_row = arow_ref[hd]
        a_prev = a_sc[hd, 0:1, 0:1]
        ct = ct_sc[hd]
        n_row = n_sc[hd, 0:1, :]
        mm = jnp.where(causal, a_row, NEG_INF)
        a_t = jnp.maximum(jnp.max(mm, axis=1, keepdims=True), a_prev)
        s = _dot_nt(q, k) * (dk ** -0.5) * jnp.exp(mm - a_t)
        inter = jnp.exp(a_prev - a_t)
        num = _dot(s.astype(BF16), v) + inter * _dot(q, ct.astype(BF16))
        den = (jnp.sum(s, axis=1, keepdims=True)
               + inter * jnp.sum(q.astype(F32) * n_row, axis=1, keepdims=True))
        h = num / jnp.maximum(jnp.abs(den), jnp.exp(-(b_col + a_t)))
        hm = _sigmoid(o_ref[:, hs].astype(F32)) * h
        hm = hm * lax.rsqrt(jnp.mean(hm * hm, axis=1, keepdims=True) + NORM_EPS) * g_ref[:, hs]
        h_ref[:, hs] = hm.astype(h_ref.dtype)

        a_end = jnp.maximum(jnp.max(a_row, axis=1, keepdims=True), a_prev)
        e_col = jnp.exp(a_col - a_end) * (dk ** -0.5)
        decay = jnp.exp(a_prev - a_end)
        ke = k.astype(F32) * e_col
        ct_new = decay * ct + _dot_tn(ke.astype(BF16), v)
        n_new = decay * n_row + jnp.sum(ke, axis=0, keepdims=True)
        ct_sc[hd] = ct_new
        n_sc[hd] = jnp.broadcast_to(n_new, n_sc.shape[1:])
        a_sc[hd] = jnp.broadcast_to(a_end, a_sc.shape[1:])
        c_ref[hd] = ct_new
        n_ref[hd] = n_new
        m_ref[hd] = jnp.broadcast_to(b_col[L - 1:L, :] + a_end, m_ref.shape[1:])

    return init, main


def _mlstm(zm, ga, gb, a_rows, c0t, n0, m0, g_norm, batch, n_heads, seq):
    L = CHUNK
    nc = seq // L
    n = batch * seq
    dh = HEAD_DIM
    hw = n_heads * dh

    def col(off):
        return pl.BlockSpec((L, hw), lambda b, c: (b * nc + c, off))

    def rows():
        return pl.BlockSpec((L, LANES), lambda b, c: (b * nc + c, 0))

    def per_b(r, w):
        return pl.BlockSpec((None, n_heads, r, w), lambda b, c: (b, 0, 0, 0))

    return dict(
        body=functools.partial(_mlstm_kernel, n_heads=n_heads),
        in_specs=[col(0), col(1), col(2), col(3), rows(), rows(),
                  pl.BlockSpec((None, n_heads, 1, L), lambda b, c: (b, 0, 0, c)),
                  per_b(dh, dh), per_b(1, dh), per_b(1, LANES),
                  pl.BlockSpec((1, hw), lambda b, c: (0, 0))],
        out_specs=[pl.BlockSpec((L, hw), lambda b, c: (b * nc + c, 0)),
                   per_b(dh, dh), per_b(1, dh), per_b(1, LANES)],
        out_shape=[jax.ShapeDtypeStruct((n, hw), BF16),
                   jax.ShapeDtypeStruct((batch, n_heads, dh, dh), F32),
                   jax.ShapeDtypeStruct((batch, n_heads, 1, dh), F32),
                   jax.ShapeDtypeStruct((batch, n_heads, 1, LANES), F32)],
        scratch_shapes=[pltpu.VMEM((n_heads, dh, dh), F32), pltpu.VMEM((n_heads, SUBLANES, dh), F32),
                        pltpu.VMEM((n_heads, SUBLANES, LANES), F32)],
        args=(zm, zm, zm, zm, ga, gb, a_rows, c0t, n0, m0, g_norm))


def _recurrent_mixers(parts, batch, seq):
    n_in = [len(p["in_specs"]) for p in parts]
    n_out = [len(p["out_specs"]) for p in parts]
    n_scr = [len(p["scratch_shapes"]) for p in parts]

    def kernel(*refs):
        ins, outs, scrs = refs[:sum(n_in)], refs[sum(n_in):sum(n_in) + sum(n_out)], refs[sum(n_in) + sum(n_out):]
        i = o = s = 0
        stages = []
        for p, ni, no, ns in zip(parts, n_in, n_out, n_scr):
            stages.append(p["body"](*ins[i:i + ni], *outs[o:o + no], *scrs[s:s + ns]))
            i, o, s = i + ni, o + no, s + ns

        @pl.when(pl.program_id(1) == 0)
        def _():
            for init, _ in stages:
                init()

        for _, main in stages:
            main()

    res = pl.pallas_call(
        kernel,
        grid=(batch, seq // CHUNK),
        in_specs=[sp for p in parts for sp in p["in_specs"]],
        out_specs=[sp for p in parts for sp in p["out_specs"]],
        out_shape=[sh for p in parts for sh in p["out_shape"]],
        scratch_shapes=[sc for p in parts for sc in p["scratch_shapes"]],
        compiler_params=_params("parallel", "arbitrary"),
        name="recurrent_mixers",
    )(*[a for p in parts for a in p["args"]])
    out, o = [], 0
    for no in n_out:
        out.append(res[o:o + no])
        o += no
    return out


def _gla_levels(L):
    levels = []
    w = L // 2
    while w >= 1:
        levels.append(w)
        w //= 2
    return levels


def _gla_consts(L):
    t = np.arange(L)
    row, colj = t[:, None], t[None, :]
    mats = [(colj <= row), (colj > row)]
    masks = []
    for w in _gla_levels(L):
        mid = (t // (2 * w)) * 2 * w + w
        right = t >= mid
        mr = right[:, None] & (colj >= mid[:, None]) & (colj <= row)
        ml = (~right)[:, None] & (colj > row) & (colj < mid[:, None])
        mats.append(mr | ml)
        same = (t[:, None] // (2 * w)) == (t[None, :] // (2 * w))
        masks.append(same & right[:, None] & (~right)[None, :])
    masks.append(row == colj)
    m_all = np.concatenate([m.astype(np.float32) for m in mats], axis=0)
    return m_all, np.stack([m.astype(np.float32) for m in masks])


def _gla_kernel(q_ref, k_ref, v_ref, r_ref, ga_ref, wa_ref, ba_ref, mall_ref, mask_ref, s0_ref, g_ref,
                h_ref, s_ref, st_sc, *, n_valid):
    L = q_ref.shape[0]
    dk = LANES // 2
    dv = HEAD_DIM
    n_lev = mask_ref.shape[0] - 1
    n_pair = st_sc.shape[0]

    def init():
        st_sc[...] = s0_ref[...]

    def main():
        ga = ga_ref[...].astype(BF16)
        mall = mall_ref[...]
        lane = lax.broadcasted_iota(jnp.int32, (1, LANES), 1)
        lm = [(lane < dk).astype(F32), (lane >= dk).astype(F32)]
        for p in range(n_pair):
            pair(p, ga, mall, lm)

    def pair(p, ga, mall, lm):
        ps = slice(p * LANES, (p + 1) * LANES)
        q2 = q_ref[:, ps].astype(F32) * (dk ** -0.5)
        k2 = k_ref[:, ps].astype(F32)
        la = _log_sigmoid(_dot(ga, wa_ref[p]) + ba_ref[p]) * (1.0 / GLA_TAU)
        if n_valid < L:
            valid = lax.broadcasted_iota(jnp.int32, (L, LANES), 0) < n_valid
            la = jnp.where(valid, la, 0.0)
            k2 = jnp.where(valid, k2, 0.0)
        la_hi = la.astype(BF16)
        la_mid = (la - la_hi.astype(F32)).astype(BF16)
        e = jnp.exp(_dot(mall, la_hi) + _dot(mall, la_mid))

        q_in = q2 * e[0:L]
        k_end = (k2 * e[L:2 * L]).astype(BF16)
        decay = e[L - 1:L]
        kq = [(q2 * e[(2 + i) * L:(3 + i) * L], (k2 * e[(2 + i) * L:(3 + i) * L]).astype(BF16))
              for i in range(n_lev)]
        k2b = k2.astype(BF16)
        for hh in range(2):
            hs = slice((2 * p + hh) * dv, (2 * p + hh + 1) * dv)
            v = v_ref[:, hs]
            st = st_sc[p, hh]
            o = _dot_nt((q_in * lm[hh]).astype(BF16), st.astype(BF16))
            a = mask_ref[n_lev] * _dot_nt((q2 * lm[hh]).astype(BF16), k2b)
            for i in range(n_lev):
                qh, kh = kq[i]
                a = a + mask_ref[i] * _dot_nt((qh * lm[hh]).astype(BF16), kh)
            o = o + _dot(a.astype(BF16), v)
            st_new = decay * st + _dot_tn(v, k_end)
            st_sc[p, hh] = st_new
            s_ref[p, hh] = st_new
            o = o * lax.rsqrt(jnp.mean(o * o, axis=1, keepdims=True) + NORM_EPS) * g_ref[:, hs]
            r = r_ref[:, hs].astype(F32)
            h_ref[:, hs] = (r * _sigmoid(r) * o).astype(h_ref.dtype)

    return init, main


def _gla(zg, ga, wa, ba, s0t, g_norm, batch, n_heads, seq, n_valid):
    L = CHUNK
    nc = seq // L
    n = batch * seq
    npair = n_heads // 2
    dv = HEAD_DIM
    m_all, masks = _gla_consts(L)
    m_all = jnp.asarray(m_all, BF16)
    masks = jnp.asarray(masks, F32)
    qkw = npair * LANES
    state = pl.BlockSpec((None, npair, 2, dv, LANES), lambda b, c: (b, 0, 0, 0, 0))

    return dict(
        body=functools.partial(_gla_kernel, n_valid=n_valid),
        in_specs=[pl.BlockSpec((L, qkw), lambda b, c: (b * nc + c, 0)),
                  pl.BlockSpec((L, qkw), lambda b, c: (b * nc + c, 1)),
                  pl.BlockSpec((L, 2 * qkw), lambda b, c: (b * nc + c, 1)),
                  pl.BlockSpec((L, 2 * qkw), lambda b, c: (b * nc + c, 2)),
                  pl.BlockSpec((L, LANES), lambda b, c: (b * nc + c, 0)),
                  pl.BlockSpec(wa.shape, lambda b, c: (0, 0, 0)),
                  pl.BlockSpec(ba.shape, lambda b, c: (0, 0, 0)),
                  pl.BlockSpec(m_all.shape, lambda b, c: (0, 0)),
                  pl.BlockSpec(masks.shape, lambda b, c: (0, 0, 0)),
                  state,
                  pl.BlockSpec((1, n_heads * dv), lambda b, c: (0, 0))],
        out_specs=[pl.BlockSpec((L, n_heads * dv), lambda b, c: (b * nc + c, 0)), state],
        out_shape=[jax.ShapeDtypeStruct((n, n_heads * dv), BF16),
                   jax.ShapeDtypeStruct((batch, npair, 2, dv, LANES), F32)],
        scratch_shapes=[pltpu.VMEM((npair, 2, dv, LANES), F32)],
        args=(zg, zg, zg, zg, ga, wa, ba, m_all, masks, s0t, g_norm))


def _fox_prompt_kernel(qi_ref, ki_ref, q_ref, k_ref, v_ref, gb_ref, crow_ref, o_ref, m_sc, acc_sc, ccol_sc,
                       *, n_heads, lane0):
    g = pl.program_id(0)
    p = pl.program_id(1)
    qi = qi_ref[p]
    ki = ki_ref[p]
    tq = q_ref.shape[0]
    tk = k_ref.shape[0]
    dh = HEAD_DIM
    hpg = q_ref.shape[1] // dh
    groups_per_batch = n_heads // hpg

    @pl.when(ki == 0)
    def _():
        m_sc[...] = jnp.full_like(m_sc, NEG_INF)
        acc_sc[...] = jnp.zeros_like(acc_sc)
        lane = lax.broadcasted_iota(jnp.int32, (tq, LANES), 1)
        gb = gb_ref[...]
        for hh in range(hpg):
            hd = (g % groups_per_batch) * hpg + hh
            ccol_sc[hh] = LOG2E * jnp.sum(jnp.where(lane == lane0 + hd, gb, 0.0), axis=1, keepdims=True)

    def step(diagonal):
        for hh in range(hpg):
            hs = slice(hh * dh, (hh + 1) * dh)
            u = _dot_nt(q_ref[:, hs], k_ref[:, hs]) - LOG2E * crow_ref[hh]
            if diagonal:
                ti = lax.broadcasted_iota(jnp.int32, (tq, tk), 0)
                si = lax.broadcasted_iota(jnp.int32, (tq, tk), 1)
                u = jnp.where(si <= ti, u, NEG_INF)
            c2 = ccol_sc[hh]
            m_old = m_sc[hh]
            m_new = jnp.maximum(m_old, jnp.max(u, axis=1, keepdims=True) + c2)
            pr = jnp.exp2(u - (m_new - c2))
            v1 = jnp.concatenate([v_ref[:, hs], jnp.ones((tk, LANES), BF16)], axis=1)
            acc_sc[hh] = jnp.exp2(m_old - m_new) * acc_sc[hh] + _dot(pr.astype(BF16), v1)
            m_sc[hh] = m_new

    @pl.when(ki < qi)
    def _():
        step(False)

    @pl.when(ki == qi)
    def _():
        step(True)
        for hh in range(hpg):
            acc = acc_sc[hh]
            o_ref[:, hh * dh:(hh + 1) * dh] = (acc[:, :dh] / acc[:, dh:]).astype(o_ref.dtype)


def _fox_prompt(zq, zk, zv, gb, c_rows, batch, n_heads, seq, tile, lane0):
    n = batch * seq
    nq = seq // tile
    dh = HEAD_DIM
    hpg = ATTN_HEADS_PER_STEP
    gpb = n_heads // hpg
    pairs = [(a, b) for a in range(nq) for b in range(a + 1)]
    qi = jnp.asarray([a for a, _ in pairs], jnp.int32)
    ki = jnp.asarray([b for _, b in pairs], jnp.int32)
    c_rows = c_rows.reshape(batch * gpb, hpg, 1, seq)

    def qrow(g, p, qi, ki):
        return (g // gpb) * nq + qi[p]

    def krow(g, p, qi, ki):
        return (g // gpb) * nq + ki[p]

    grid_spec = pltpu.PrefetchScalarGridSpec(
        num_scalar_prefetch=2,
        grid=(batch * gpb, len(pairs)),
        in_specs=[pl.BlockSpec((tile, hpg * dh), lambda g, p, qi, ki: (qrow(g, p, qi, ki), g % gpb)),
                  pl.BlockSpec((tile, hpg * dh), lambda g, p, qi, ki: (krow(g, p, qi, ki), g % gpb)),
                  pl.BlockSpec((tile, hpg * dh), lambda g, p, qi, ki: (krow(g, p, qi, ki), g % gpb)),
                  pl.BlockSpec((tile, LANES), lambda g, p, qi, ki: (qrow(g, p, qi, ki), 0)),
                  pl.BlockSpec((None, hpg, 1, tile), lambda g, p, qi, ki: (g, 0, 0, ki[p]))],
        out_specs=pl.BlockSpec((tile, hpg * dh), lambda g, p, qi, ki: (qrow(g, p, qi, ki), g % gpb)),
        scratch_shapes=[pltpu.VMEM((hpg, tile, 1), F32), pltpu.VMEM((hpg, tile, dh + LANES), F32),
                        pltpu.VMEM((hpg, tile, 1), F32)],
    )
    return pl.pallas_call(
        functools.partial(_fox_prompt_kernel, n_heads=n_heads, lane0=lane0),
        grid_spec=grid_spec,
        out_shape=jax.ShapeDtypeStruct((n, n_heads * dh), BF16),
        compiler_params=_params("parallel", "arbitrary"),
        name="fox_prompt",
    )(qi, ki, zq, zk, zv, gb, c_rows)


def _fox_sample_kernel(pt_ref, q_ref, cq_ref, kn_ref, vn_ref, bn_ref, *rest, n_heads, n_new, group):
    k_refs, v_refs, lf_refs = rest[:group], rest[group:2 * group], rest[2 * group:3 * group]
    o_ref, m_sc, l_sc, acc_sc, carry_sc = rest[3 * group:]
    j = pl.program_id(1)
    nr = q_ref.shape[0]
    rows_pp = k_refs[0].shape[0]
    d = q_ref.shape[1]
    n_tiles = rows_pp // LANES
    scale = d ** -0.5

    @pl.when(j == 0)
    def _():
        m_sc[...] = jnp.full_like(m_sc, NEG_INF)
        l_sc[...] = jnp.zeros_like(l_sc)
        acc_sc[...] = jnp.zeros_like(acc_sc)
        carry_sc[...] = jnp.zeros_like(carry_sc)

    lane8 = lax.broadcasted_iota(jnp.int32, (SUBLANES, LANES), 1)
    row8 = lax.broadcasted_iota(jnp.int32, (SUBLANES, LANES), 0)

    def page_suffix(x):
        y = x
        z = x
        sh = n_heads
        while sh < LANES:
            y = y + jnp.where(lane8 + sh < LANES, pltpu.roll(y, LANES - sh, 1), 0.0)
            z = z + pltpu.roll(z, sh, 1)
            sh *= 2
        w = z
        sh = 1
        while sh < SUBLANES:
            w = w + jnp.where(row8 + sh < SUBLANES, pltpu.roll(w, SUBLANES - sh, 0), 0.0)
            sh *= 2
        return y - x + (w - z), jnp.broadcast_to(w[0:1, :], x.shape)

    q = q_ref[...]
    cq = cq_ref[...]
    rowi = lax.broadcasted_iota(jnp.int32, (nr, LANES), 0)
    lanei = lax.broadcasted_iota(jnp.int32, (nr, LANES), 1)
    head_ok = (rowi // n_new) == (lanei % n_heads)

    def update(tile_groups, vbs):
        flat = [t for ts in tile_groups for t in ts]
        m_old = m_sc[...]
        mx = flat[0]
        for t in flat[1:]:
            mx = jnp.maximum(mx, t)
        m_new = jnp.maximum(m_old, jnp.max(mx, axis=1, keepdims=True))
        alpha = jnp.exp(m_old - m_new)
        tot = None
        acc = alpha * acc_sc[...]
        for ts, vb in zip(tile_groups, vbs):
            ps = [jnp.exp(t - m_new) for t in ts]
            for t in ps:
                tot = t if tot is None else tot + t
            pcat = ps[0] if len(ps) == 1 else jnp.concatenate(ps, axis=1)
            acc = acc + _dot(pcat.astype(BF16), vb)
        l_sc[...] = alpha * l_sc[...] + jnp.sum(tot, axis=1, keepdims=True)
        acc_sc[...] = acc
        m_sc[...] = m_new

    carry = carry_sc[...]
    tile_groups = []
    for g in range(group):
        within, total = page_suffix(lf_refs[g][...])
        suffix = within + carry
        carry = carry + total
        s = _dot_nt(q, k_refs[g][...].astype(BF16)) * scale
        tile_groups.append([jnp.where(head_ok, s[:, r * LANES:(r + 1) * LANES] + cq + suffix[r:r + 1, :], NEG_INF)
                            for r in range(n_tiles)])
    carry_sc[...] = carry
    update(tile_groups, [v_refs[g][...].astype(BF16) for g in range(group)])

    @pl.when(j == pl.num_programs(1) - 1)
    def _():
        s_self = _dot_nt(q, kn_ref[...]) * scale + cq + bn_ref[...]
        ok = head_ok & (lanei < n_new * n_heads) & ((lanei // n_heads) <= (rowi % n_new))
        update([[jnp.where(ok, s_self, NEG_INF)]], [vn_ref[...]])
        o_ref[...] = acc_sc[...] / l_sc[...]


def _fox_sample(page_table, q, cq, kn, vn, bn, cache_k, cache_v, cache_lf, layer, n_heads, n_new):
    nb, n_pages = page_table.shape
    nr = q.shape[1]
    rows_pp = cache_k.shape[2]
    d = q.shape[2]
    group = max(g for g in (1, 2, 4, 8, 16) if n_pages % g == 0)

    def page_spec(rows, width, g):
        return pl.BlockSpec((None, None, rows, width),
                            lambda b, j, pt: (layer, pt[b, n_pages - 1 - (j * group + g)], 0, 0))

    per_b = lambda rows, width: pl.BlockSpec((None, rows, width), lambda b, j, pt: (b, 0, 0))
    grid_spec = pltpu.PrefetchScalarGridSpec(
        num_scalar_prefetch=1,
        grid=(nb, n_pages // group),
        in_specs=([per_b(nr, d), per_b(nr, LANES), per_b(LANES, d), per_b(LANES, d), per_b(1, LANES)]
                  + [page_spec(rows_pp, d, g) for g in range(group)]
                  + [page_spec(rows_pp, d, g) for g in range(group)]
                  + [page_spec(SUBLANES, LANES, g) for g in range(group)]),
        out_specs=per_b(nr, d),
        scratch_shapes=[pltpu.VMEM((nr, 1), F32), pltpu.VMEM((nr, 1), F32), pltpu.VMEM((nr, d), F32),
                        pltpu.VMEM((SUBLANES, LANES), F32)],
    )
    return pl.pallas_call(
        functools.partial(_fox_sample_kernel, n_heads=n_heads, n_new=n_new, group=group),
        grid_spec=grid_spec,
        out_shape=jax.ShapeDtypeStruct((nb, nr, d), F32),
        compiler_params=_params("parallel", "arbitrary"),
        name="fox_sample",
    )(page_table, q, cq, kn, vn, bn, *([cache_k] * group), *([cache_v] * group), *([cache_lf] * group))


def _layer_norm(r, g, b):
    mu = jnp.mean(r, axis=1, keepdims=True)
    xc = r - mu
    var = jnp.mean(xc * xc, axis=1, keepdims=True)
    return xc * lax.rsqrt(var + LN_EPS) * g + b


def _outproj_kernel(hm_ref, hg_ref, hf_ref, wm_ref, wg_ref, wf_ref, x_ref, g_ref, b_ref, h_ref, hb_ref, *, alpha):
    mix = _dot(hm_ref[...], wm_ref[...]) + _dot(hg_ref[...], wg_ref[...]) + _dot(hf_ref[...], wf_ref[...])
    h = _layer_norm(alpha * x_ref[...] + mix, g_ref[...], b_ref[...])
    h_ref[...] = h
    hb_ref[...] = h.astype(BF16)


def _outproj_ln(hm, hg, hf, w_out, x, g, b, alpha, tm):
    n, d = x.shape
    tm = min(tm, n)
    wm, wg, wf = hm.shape[1], hg.shape[1], hf.shape[1]
    assert wm == wg and wf == wm + wg and n % tm == 0
    const = pl.Buffered(1)
    return pl.pallas_call(
        functools.partial(_outproj_kernel, alpha=alpha),
        grid=(n // tm,),
        in_specs=[pl.BlockSpec((tm, wm), lambda i: (i, 0)),
                  pl.BlockSpec((tm, wg), lambda i: (i, 0)),
                  pl.BlockSpec((tm, wf), lambda i: (i, 0)),
                  pl.BlockSpec((wm, d), lambda i: (0, 0), pipeline_mode=const),
                  pl.BlockSpec((wg, d), lambda i: (1, 0), pipeline_mode=const),
                  pl.BlockSpec((wf, d), lambda i: (1, 0), pipeline_mode=const),
                  pl.BlockSpec((tm, d), lambda i: (i, 0)),
                  pl.BlockSpec((1, d), lambda i: (0, 0)),
                  pl.BlockSpec((1, d), lambda i: (0, 0))],
        out_specs=[pl.BlockSpec((tm, d), lambda i: (i, 0))] * 2,
        out_shape=[jax.ShapeDtypeStruct((n, d), F32), jax.ShapeDtypeStruct((n, d), BF16)],
        compiler_params=_params("parallel"),
        name="outproj_ln",
    )(hm, hg, hf, w_out, w_out, w_out, x, g, b)


def _ffn_up_kernel(x_ref, wg_ref, wv_ref, cwg_ref, cwv_ref, cbg_ref, cbv_ref, hg0_ref, hv0_ref,
                   o_ref, tg_ref, tv_ref, halo_g, halo_v, win_g, win_v, *, tiles_per_seq, shift):
    i = pl.program_id(0)
    j = pl.program_id(1)
    tm = x_ref.shape[0]
    hr = hg0_ref.shape[0]
    tn = o_ref.shape[1]

    @pl.when(i % tiles_per_seq == 0)
    def _():
        halo_g[j] = hg0_ref[...]
        halo_v[j] = hv0_ref[...]

    rc = min(FFN_ROW_CHUNK, tm)

    rb = min(FFN_EPILOGUE_ROWS, rc)

    def conv(r0, cs, cw_ref, cb_ref, win):
        cw = cw_ref[:, cs]
        return (cb_ref[:, cs] + cw[0:1] * win[pl.ds(hr + r0 - 2 * shift, rb), cs]
                + cw[1:2] * win[pl.ds(hr + r0 - shift, rb), cs] + cw[2:3] * win[pl.ds(hr + r0, rb), cs])

    win_g[0:hr, :] = halo_g[j]
    win_v[0:hr, :] = halo_v[j]
    for c0 in range(0, tn, FFN_COL_SLAB):
        cs = slice(c0, c0 + FFN_COL_SLAB)
        for r0 in range(0, tm, rc):
            rs = slice(r0, r0 + rc)
            ug = _dot(x_ref[rs, :], wg_ref[:, cs])
            uv = _dot(x_ref[rs, :], wv_ref[:, cs])
            win_g[hr + r0:hr + r0 + rc, cs] = ug
            win_v[hr + r0:hr + r0 + rc, cs] = uv
            for r1 in range(r0, r0 + rc, rb):
                yg = conv(r1, cs, cwg_ref, cbg_ref, win_g)
                yv = conv(r1, cs, cwv_ref, cbv_ref, win_v)
                o_ref[r1:r1 + rb, cs] = (yg * _sigmoid(yg) * yv).astype(o_ref.dtype)
    tail_g = win_g[tm:tm + hr, :]
    tail_v = win_v[tm:tm + hr, :]
    tg_ref[...] = tail_g
    tv_ref[...] = tail_v
    halo_g[j] = tail_g
    halo_v[j] = tail_v


def _ffn_up(x, w_up, conv_w, conv_b, halo0, fp, tm, tn, tiles_per_seq, shift):
    n, d = x.shape
    tm = min(tm, n)
    nj = fp // tn
    ni = n // tm
    hr = halo0.shape[0] // (ni // tiles_per_seq)
    assert n % tm == 0 and fp % tn == 0 and hr >= 2 * shift
    seq = lambda i: i // tiles_per_seq
    return pl.pallas_call(
        functools.partial(_ffn_up_kernel, tiles_per_seq=tiles_per_seq, shift=shift),
        grid=(ni, nj),
        in_specs=[pl.BlockSpec((tm, d), lambda i, j: (i, 0)),
                  pl.BlockSpec((d, tn), lambda i, j: (0, j)),
                  pl.BlockSpec((d, tn), lambda i, j: (0, nj + j)),
                  pl.BlockSpec((CONV_W, tn), lambda i, j: (0, j)),
                  pl.BlockSpec((CONV_W, tn), lambda i, j: (0, nj + j)),
                  pl.BlockSpec((1, tn), lambda i, j: (0, j)),
                  pl.BlockSpec((1, tn), lambda i, j: (0, nj + j)),
                  pl.BlockSpec((hr, tn), lambda i, j: (seq(i), j)),
                  pl.BlockSpec((hr, tn), lambda i, j: (seq(i), nj + j))],
        out_specs=[pl.BlockSpec((tm, tn), lambda i, j: (i, j)),
                   pl.BlockSpec((hr, tn), lambda i, j: (i, j)),
                   pl.BlockSpec((hr, tn), lambda i, j: (i, j))],
        out_shape=[jax.ShapeDtypeStruct((n, fp), BF16),
                   jax.ShapeDtypeStruct((ni * hr, fp), F32),
                   jax.ShapeDtypeStruct((ni * hr, fp), F32)],
        scratch_shapes=[pltpu.VMEM((nj, hr, tn), F32), pltpu.VMEM((nj, hr, tn), F32),
                        pltpu.VMEM((hr + tm, tn), F32), pltpu.VMEM((hr + tm, tn), F32)],
        compiler_params=_params("arbitrary", "arbitrary"),
        name="ffn_up",
    )(x, w_up, w_up, conv_w, conv_w, conv_b, conv_b, halo0, halo0)


def _ffn_down_kernel(a_ref, w_ref, h_ref, g_ref, b_ref, x_ref, xb_ref, *, alpha):
    x = _layer_norm(alpha * h_ref[...] + _dot(a_ref[...], w_ref[...]), g_ref[...], b_ref[...])
    x_ref[...] = x
    xb_ref[...] = x.astype(BF16)


def _ffn_down_ln(a, w_down, h, g, b, alpha, tm):
    n, fp = a.shape
    d = w_down.shape[1]
    tm = min(tm, n)
    assert n % tm == 0
    return pl.pallas_call(
        functools.partial(_ffn_down_kernel, alpha=alpha),
        grid=(n // tm,),
        in_specs=[pl.BlockSpec((tm, fp), lambda i: (i, 0)),
                  pl.BlockSpec((fp, d), lambda i: (0, 0), pipeline_mode=pl.Buffered(1)),
                  pl.BlockSpec((tm, d), lambda i: (i, 0)),
                  pl.BlockSpec((1, d), lambda i: (0, 0)),
                  pl.BlockSpec((1, d), lambda i: (0, 0))],
        out_specs=[pl.BlockSpec((tm, d), lambda i: (i, 0))] * 2,
        out_shape=[jax.ShapeDtypeStruct((n, d), F32), jax.ShapeDtypeStruct((n, d), BF16)],
        compiler_params=_params("parallel"),
        name="ffn_down_ln",
    )(a, w_down, h, g, b)


def _prep_layer(l, dims, w_in, b_m_ig, b_m_fg, w_g_alpha_up, b_g_alpha, b_f, g_m_norm, g_g_norm, w_out,
                ln1_g, ln1_b, w_up, conv_w, conv_b, w_down, ln2_g, ln2_b):
    d, hm, hg, hf, f, fp = dims
    dh = HEAD_DIM
    dkg = dh // 2
    sizes = [hm * dh] * 4 + [hm, hm] + [hg * dkg] * 2 + [hg * dh] * 2 + [GLA_RANK] + [hf * dh] * 3 + [hf]
    offs = np.concatenate([[0], np.cumsum(sizes)]).tolist()
    (o_mq, o_mk, o_mv, o_mo, o_mi, o_mf, o_gq, o_gk, o_gv, o_gr, o_ga, o_fq, o_fk, o_fv, o_ff, _) = offs
    w = w_in[l]
    col = lambda a, b: w[:, a:b]
    zeros = lambda c: jnp.zeros((d, c), F32)
    blk_a = jnp.concatenate([col(o_mi, o_mi + hm), zeros(8 - hm), col(o_ga, o_ga + GLA_RANK),
                             zeros(LANES - 8 - GLA_RANK)], axis=1)
    blk_b = jnp.concatenate([col(o_mf, o_mf + hm), zeros(8 - hm), col(o_ff, o_ff + hf),
                             zeros(LANES - 8 - hf)], axis=1)
    pad1 = lambda v, lo, total: jnp.pad(v, (lo, total - lo - v.shape[0]))[None, :]
    wa = jnp.zeros((LANES, hg * dkg), F32).at[8:8 + GLA_RANK].set(w_g_alpha_up[l])
    npair = hg // 2
    pad_f = lambda a: jnp.pad(a, [(0, 0)] * (a.ndim - 1) + [(0, fp - f)])
    return dict(
        w_m=col(o_mq, o_mi).astype(BF16),
        w_g=col(o_gq, o_ga).astype(BF16),
        w_fq=col(o_fq, o_fk).astype(BF16),
        w_fk=col(o_fk, o_fv).astype(BF16),
        w_fv=col(o_fv, o_ff).astype(BF16),
        w_small=jnp.concatenate([blk_a, blk_b], axis=1).astype(BF16),
        bias_a=pad1(b_m_ig[l], 0, LANES),
        bias_b=pad1(b_m_fg[l], 0, LANES) + pad1(b_f[l], 8, LANES),
        wa=wa.reshape(LANES, npair, LANES).transpose(1, 0, 2).astype(BF16),
        ba=b_g_alpha[l].reshape(npair, 1, LANES),
        g_m=g_m_norm[l][None, :], g_g=g_g_norm[l][None, :],
        w_out=w_out[l].astype(BF16),
        ln1_g=ln1_g[l][None, :], ln1_b=ln1_b[l][None, :], ln2_g=ln2_g[l][None, :], ln2_b=ln2_b[l][None, :],
        w_up=_cast_pad_cols(w_up, l, 2, fp, 256),
        conv_w=jnp.concatenate([pad_f(conv_w[l][:, :f]), pad_f(conv_w[l][:, f:])], axis=1),
        conv_b=jnp.concatenate([pad_f(conv_b[l][:f]), pad_f(conv_b[l][f:])])[None, :],
        w_down=_cast_pad_rows(w_down, l, fp, 512),
    )


def _pad_conv_state(s, f, fp):
    pad = [(0, 0)] * (s.ndim - 1) + [(0, fp - f)]
    return jnp.concatenate([jnp.pad(s[..., :f], pad), jnp.pad(s[..., f:], pad)], axis=-1)


def _mixer_rows(x_bf, p, tm, q_scale=1.0):
    zm, = _matmul(x_bf, p["w_m"], [BF16], tm, 512, "proj_m")
    zg, = _matmul(x_bf, p["w_g"], [BF16], tm, 512, "proj_g")
    zq, = _matmul(x_bf, p["w_fq"], [BF16], tm, 512, "proj_fq", scale=q_scale)
    zk, zkb = _matmul(x_bf, p["w_fk"], [F32, BF16], tm, 512, "proj_fk")
    zv, zvb = _matmul(x_bf, p["w_fv"], [F32, BF16], tm, 512, "proj_fv")
    return zm, zg, zq, zk, zkb, zv, zvb


def kernel(x_prompt, x_sample, state_mlstm_c, state_mlstm_n, state_mlstm_m, state_gla, state_ffn_conv,
           cache_k, cache_v, cache_logf, page_table,
           w_in, b_m_ig, b_m_fg, w_g_alpha_up, b_g_alpha, b_f, g_m_norm, g_g_norm, w_out,
           ln1_g, ln1_b, w_up, conv_w, conv_b, w_down, ln2_g, ln2_b):
    bp, seq, d = x_prompt.shape
    db, ns, _ = x_sample.shape
    depth = w_in.shape[0]
    hm = b_m_ig.shape[1]
    hf = b_f.shape[1]
    dh = HEAD_DIM
    hg = g_g_norm.shape[1] // dh
    dkg = dh // 2
    f = w_down.shape[1]
    fp = -(-f // 512) * 512
    alpha = (2.0 * depth) ** 0.25
    n_pool, page = cache_k.shape[1], cache_k.shape[2]
    assert hm <= 8 and hf <= 8 and hg % 2 == 0 and seq % 512 == 0 and ns <= CHUNK
    assert page * hf == SUBLANES * LANES and ns * hf <= LANES
    dims = (d, hm, hg, hf, f, fp)
    L = CHUNK
    n_p = bp * seq

    tri_p = jnp.asarray(np.tril(np.ones((GATE_TILE, GATE_TILE), np.float32)), BF16)
    gs_rows = LANES
    r = np.arange(gs_rows)
    tri_s_np = ((r[:, None] % db == r[None, :] % db) & (r[None, :] // db <= r[:, None] // db)
                & (r[:, None] < ns * db) & (r[None, :] < ns * db))
    tri_s = jnp.asarray(tri_s_np.astype(np.float32), BF16)

    xp = x_prompt.astype(F32).reshape(n_p, d)
    xp_bf = xp.astype(BF16)
    xs = x_sample.astype(F32).transpose(1, 0, 2).reshape(ns * db, d)
    xs_bf = xs.astype(BF16)

    ck = cache_k.astype(F32).reshape(depth, n_pool, page * hf, dh)
    cv = cache_v.astype(F32).reshape(depth, n_pool, page * hf, dh)
    clf = cache_logf.astype(F32).reshape(depth, n_pool, SUBLANES, LANES)

    def to_padded(z, pad_value=0.0, mode="constant"):
        c = z.shape[-1]
        a = z.reshape(ns, db, c).transpose(1, 0, 2)
        if mode == "edge":
            a = jnp.pad(a, ((0, 0), (0, L - ns), (0, 0)), mode="edge")
        else:
            a = jnp.pad(a, ((0, 0), (0, L - ns), (0, 0)), constant_values=pad_value)
        return a.reshape(db * L, c)

    def from_padded(y):
        c = y.shape[-1]
        return y.reshape(db, L, c)[:, :ns].transpose(1, 0, 2).reshape(ns * db, c)

    def head_rows(g, lane0, nh, batch, t):
        return g[:, lane0:lane0 + nh].reshape(batch, t, nh).transpose(0, 2, 1).reshape(batch * nh, 1, t)

    outs_p = [[] for _ in range(8)]
    outs_s = [[] for _ in range(8)]
    for l in range(depth):
        p = _prep_layer(l, dims, w_in, b_m_ig, b_m_fg, w_g_alpha_up, b_g_alpha, b_f, g_m_norm, g_g_norm, w_out,
                        ln1_g, ln1_b, w_up, conv_w, conv_b, w_down, ln2_g, ln2_b)

        zm, zg, zq, zk, zkb, zv, zvb = _mixer_rows(xp_bf, p, 1024, q_scale=LOG2E * dh ** -0.5)
        ga, gb, gc = _gates(xp_bf, p["w_small"], p["bias_a"], p["bias_b"], tri_p, seq // GATE_TILE, hm)
        zeros = lambda *s: jnp.zeros(s, F32)
        (h_g, st), (h_m, ct, n_m, m_m) = _recurrent_mixers(
            [_gla(zg, ga, p["wa"], p["ba"], zeros(bp, hg // 2, 2, dh, LANES), p["g_g"], bp, hg, seq, L),
             _mlstm(zm, ga, gb, head_rows(ga, 0, hm, bp, seq).reshape(bp, hm, 1, seq),
                    zeros(bp, hm, dh, dh), zeros(bp, hm, 1, dh), zeros(bp, hm, 1, LANES), p["g_m"], bp, hm, seq)],
            bp, seq)
        h_f = _fox_prompt(zq, zkb, zvb, gb, head_rows(gb, 8, hf, bp, seq), bp, hf, seq, ATTN_TILE, 8)
        h, h_bf = _outproj_ln(h_m, h_g, h_f, p["w_out"], xp, p["ln1_g"], p["ln1_b"], alpha, 512)
        tm_f = min(FFN_ROW_TILE, seq)
        act, tg, tv = _ffn_up(h_bf, p["w_up"], p["conv_w"], p["conv_b"], zeros(bp * SUBLANES, 2 * fp),
                              fp, tm_f, 512, seq // tm_f, 1)
        xp, xp_bf = _ffn_down_ln(act, p["w_down"], h, p["ln2_g"], p["ln2_b"], alpha, FFN_DOWN_ROW_TILE)

        def conv_tail_p(t):
            t = t.reshape(bp, seq // tm_f, SUBLANES, fp)[:, -1, SUBLANES - (CONV_W - 1):, :f]
            return t
        outs_p[0].append(ct.reshape(bp, hm, dh, dh).swapaxes(-1, -2))
        outs_p[1].append(n_m.reshape(bp, hm, dh))
        outs_p[2].append(m_m[:, :, 0, 0])
        st_h = st.reshape(bp, hg // 2, 2, dh, 2, dkg)
        st_h = jnp.stack([st_h[:, :, 0, :, 0, :], st_h[:, :, 1, :, 1, :]], axis=2)
        outs_p[3].append(st_h.reshape(bp, hg, dh, dkg).swapaxes(-1, -2))
        outs_p[4].append(jnp.concatenate([conv_tail_p(tg), conv_tail_p(tv)], axis=-1))
        outs_p[5].append(zk.reshape(bp, seq, hf, dh))
        outs_p[6].append(zv.reshape(bp, seq, hf, dh))
        outs_p[7].append(gc[:, 8:8 + hf].reshape(bp, seq, hf))

        n_s = ns * db
        zm, zg, zq, zk, zkb, zv, zvb = _mixer_rows(xs_bf, p, n_s)
        xs_pad = jnp.pad(xs_bf, ((0, gs_rows - n_s), (0, 0)))
        ga, gb, gc = _gates(xs_pad, p["w_small"], p["bias_a"], p["bias_b"], tri_s, 1, hm)
        ga, gb, gc = ga[:n_s], gb[:n_s], gc[:n_s]
        ga_p = to_padded(ga, NEG_INF)
        gb_p = to_padded(gb, mode="edge")
        c0t = state_mlstm_c[l].astype(F32).swapaxes(-1, -2)
        n0 = state_mlstm_n[l].astype(F32).reshape(db, hm, 1, dh)
        m0 = jnp.broadcast_to(state_mlstm_m[l].astype(F32).reshape(db, hm, 1, 1), (db, hm, 1, LANES))
        s0 = state_gla[l].astype(F32).swapaxes(-1, -2).reshape(db, hg // 2, 2, dh, dkg)
        s0t = jnp.stack([jnp.pad(s0[:, :, 0], ((0, 0), (0, 0), (0, 0), (0, dkg))),
                         jnp.pad(s0[:, :, 1], ((0, 0), (0, 0), (0, 0), (dkg, 0)))], axis=2)
        (h_m, ct, n_m, m_m), (h_g, st) = _recurrent_mixers(
            [_mlstm(to_padded(zm), ga_p, gb_p, head_rows(ga_p, 0, hm, db, L).reshape(db, hm, 1, L),
                    c0t, n0, m0, p["g_m"], db, hm, L),
             _gla(to_padded(zg), to_padded(ga), p["wa"], p["ba"], s0t, p["g_g"], db, hg, L, ns)],
            db, L)

        q_s = zq.reshape(ns, db, hf, dh).transpose(1, 2, 0, 3).reshape(db, hf * ns, dh)
        new_rows = lambda z: jnp.pad(z.reshape(ns, db, hf * dh).transpose(1, 0, 2).reshape(db, ns * hf, dh),
                                     ((0, 0), (0, LANES - ns * hf), (0, 0)))
        c_new = gb[:, 8:8 + hf].reshape(ns, db, hf)
        cq = jnp.broadcast_to(c_new.transpose(1, 2, 0).reshape(db, hf * ns, 1), (db, hf * ns, LANES))
        bn = jnp.pad(-c_new.transpose(1, 0, 2).reshape(db, 1, ns * hf), ((0, 0), (0, 0), (0, LANES - ns * hf)))
        o_f = _fox_sample(page_table, q_s, cq, new_rows(zkb), new_rows(zvb), bn, ck, cv, clf, l, hf, ns)
        h_f = o_f.reshape(db, hf, ns, dh).transpose(2, 0, 1, 3).reshape(n_s, hf * dh).astype(BF16)

        h, h_bf = _outproj_ln(from_padded(h_m), from_padded(h_g), h_f, p["w_out"], xs, p["ln1_g"], p["ln1_b"],
                              alpha, n_s)
        halo_s = _pad_conv_state(state_ffn_conv[l].astype(F32).transpose(1, 0, 2).reshape((CONV_W - 1) * db, 2 * f),
                                 f, fp)
        act, tg, tv = _ffn_up(h_bf, p["w_up"], p["conv_w"], p["conv_b"], halo_s, fp, n_s, 512, 1, db)
        xs, xs_bf = _ffn_down_ln(act, p["w_down"], h, p["ln2_g"], p["ln2_b"], alpha, n_s)

        conv_tail_s = lambda t: t[:, :f].reshape(CONV_W - 1, db, f).transpose(1, 0, 2)
        outs_s[0].append(ct.reshape(db, hm, dh, dh).swapaxes(-1, -2))
        outs_s[1].append(n_m.reshape(db, hm, dh))
        outs_s[2].append(m_m[:, :, 0, 0])
        st_h = st.reshape(db, hg // 2, 2, dh, 2, dkg)
        st_h = jnp.stack([st_h[:, :, 0, :, 0, :], st_h[:, :, 1, :, 1, :]], axis=2)
        outs_s[3].append(st_h.reshape(db, hg, dh, dkg).swapaxes(-1, -2))
        outs_s[4].append(jnp.concatenate([conv_tail_s(tg), conv_tail_s(tv)], axis=-1))
        outs_s[5].append(zk.reshape(ns, db, hf, dh).transpose(1, 0, 2, 3))
        outs_s[6].append(zv.reshape(ns, db, hf, dh).transpose(1, 0, 2, 3))
        outs_s[7].append(gc[:, 8:8 + hf].reshape(ns, db, hf).transpose(1, 0, 2))

    y_p = xp.reshape(bp, seq, d)
    y_s = xs.reshape(ns, db, d).transpose(1, 0, 2)
    return (y_p, y_s) + tuple(jnp.stack(a) for a in outs_p) + tuple(jnp.stack(a) for a in outs_s)
```

```python
import functools

import numpy as np
import jax
import jax.numpy as jnp
from jax import lax
from jax.experimental import pallas as pl
from jax.experimental.pallas import tpu as pltpu

F32 = jnp.float32
BF16 = jnp.bfloat16

HEAD_DIM = 128
GLA_RANK = 16
GLA_TAU = 16.0
CONV_W = 3
LN_EPS = 1e-5
NORM_EPS = 1e-6

LANES = 128
SUBLANES = 8
VMEM_LIMIT = 56 * 1024 * 1024

CHUNK = 128
GATE_TILE = 256
ATTN_TILE = 512
ATTN_HEADS_PER_STEP = 8
FFN_ROW_TILE = 2048
FFN_COL_SLAB = 256
FFN_ROW_CHUNK = 512
FFN_EPILOGUE_ROWS = 512
FFN_DOWN_ROW_TILE = 256
NEG_INF = float("-inf")
LOG2E = 1.4426950408889634


def _params(*sem, flags=None):
    return pltpu.CompilerParams(dimension_semantics=sem, vmem_limit_bytes=VMEM_LIMIT, flags=flags)


def _dot(a, b):
    return jnp.dot(a, b, preferred_element_type=F32)


def _dot_nt(a, b):
    return lax.dot_general(a, b, (((1,), (1,)), ((), ())), preferred_element_type=F32)


def _dot_tn(a, b):
    return lax.dot_general(a, b, (((0,), (0,)), ((), ())), preferred_element_type=F32)


def _log_sigmoid(x):
    return jnp.minimum(x, 0.0) - jnp.log1p(jnp.exp(-jnp.abs(x)))


def _sigmoid(x):
    return 1.0 / (1.0 + jnp.exp(-x))


def _split3(x):
    hi = x.astype(BF16)
    r = x - hi.astype(F32)
    mid = r.astype(BF16)
    lo = (r - mid.astype(F32)).astype(BF16)
    return hi, mid, lo


def _mm_kernel(x_ref, w_ref, *o_refs, scale):
    acc = _dot(x_ref[...], w_ref[...])
    if scale != 1.0:
        acc = acc * scale
    for o in o_refs:
        o[...] = acc.astype(o.dtype)


def _matmul(x, w, out_dtypes, tm, tn, name, scale=1.0):
    m, k = x.shape
    n = w.shape[1]
    tm = min(tm, m)
    tn = min(tn, n)
    assert m % tm == 0 and n % tn == 0
    return pl.pallas_call(
        functools.partial(_mm_kernel, scale=scale),
        grid=(m // tm, n // tn),
        in_specs=[pl.BlockSpec((tm, k), lambda i, j: (i, 0)),
                  pl.BlockSpec((k, tn), lambda i, j: (0, j))],
        out_specs=[pl.BlockSpec((tm, tn), lambda i, j: (i, j)) for _ in out_dtypes],
        out_shape=[jax.ShapeDtypeStruct((m, n), d) for d in out_dtypes],
        compiler_params=_params("parallel", "parallel"),
        name=name,
    )(x, w)


def _cast_pad_rows_kernel(x_ref, o_ref):
    r = x_ref.shape[0]
    o_ref[0:r, :] = x_ref[...].astype(o_ref.dtype)
    if o_ref.shape[0] > r:
        o_ref[r:, :] = jnp.zeros((o_ref.shape[0] - r, o_ref.shape[1]), o_ref.dtype)


def _cast_pad_rows(w, layer, rows_padded, tc):
    _, r, c = w.shape
    assert r % 16 == 0 and rows_padded % 16 == 0 and c % tc == 0
    return pl.pallas_call(
        _cast_pad_rows_kernel,
        grid=(c // tc,),
        in_specs=[pl.BlockSpec((None, r, tc), lambda j: (layer, 0, j))],
        out_specs=pl.BlockSpec((rows_padded, tc), lambda j: (0, j)),
        out_shape=jax.ShapeDtypeStruct((rows_padded, c), BF16),
        compiler_params=_params("parallel"),
        name="cast_pad_rows",
    )(w)


def _cast_pad_cols_kernel(x_ref, o_ref, *, n_parts, part, part_padded):
    for p in range(n_parts):
        o_ref[:, p * part_padded:p * part_padded + part] = x_ref[:, p * part:(p + 1) * part].astype(o_ref.dtype)
        if part_padded > part:
            o_ref[:, p * part_padded + part:(p + 1) * part_padded] = jnp.zeros(
                (o_ref.shape[0], part_padded - part), o_ref.dtype)


def _cast_pad_cols(w, layer, n_parts, part_padded, tr):
    _, r, c = w.shape
    part = c // n_parts
    assert part % LANES == 0 and part_padded % LANES == 0 and r % tr == 0
    return pl.pallas_call(
        functools.partial(_cast_pad_cols_kernel, n_parts=n_parts, part=part, part_padded=part_padded),
        grid=(r // tr,),
        in_specs=[pl.BlockSpec((None, tr, c), lambda i: (layer, i, 0))],
        out_specs=pl.BlockSpec((tr, n_parts * part_padded), lambda i: (i, 0)),
        out_shape=jax.ShapeDtypeStruct((r, n_parts * part_padded), BF16),
        compiler_params=_params("parallel"),
        name="cast_pad_cols",
    )(w)


def _gates_kernel(x_ref, w_ref, ba_ref, bb_ref, tri_ref, oa_ref, ob_ref, oc_ref, carry, *, tiles_per_seq, n_ig):
    i = pl.program_id(0)
    tm = x_ref.shape[0]
    z = _dot(x_ref[...], w_ref[...])
    za = z[:, :LANES] + ba_ref[...]
    ls = _log_sigmoid(z[:, LANES:] + bb_ref[...])
    tri = tri_ref[...]
    hi, mid, lo = _split3(ls)
    cum = _dot(tri, hi) + _dot(tri, mid) + _dot(tri, lo)
    if tiles_per_seq > 1:
        @pl.when(i % tiles_per_seq == 0)
        def _():
            carry[...] = jnp.zeros_like(carry)
        cum = cum + carry[0:1, :]
        carry[...] = jnp.broadcast_to(cum[tm - 1:tm, :], carry.shape)
    lane = lax.broadcasted_iota(jnp.int32, (tm, LANES), 1)
    oa_ref[...] = jnp.where(lane < n_ig, za - cum, za)
    ob_ref[...] = cum
    oc_ref[...] = ls


def _gates(x, w_small, bias_a, bias_b, tri, tiles_per_seq, n_ig):
    n, d = x.shape
    tm = tri.shape[0]
    assert n % tm == 0
    out = jax.ShapeDtypeStruct((n, LANES), F32)
    return pl.pallas_call(
        functools.partial(_gates_kernel, tiles_per_seq=tiles_per_seq, n_ig=n_ig),
        grid=(n // tm,),
        in_specs=[pl.BlockSpec((tm, d), lambda i: (i, 0)),
                  pl.BlockSpec((d, 2 * LANES), lambda i: (0, 0)),
                  pl.BlockSpec((1, LANES), lambda i: (0, 0)),
                  pl.BlockSpec((1, LANES), lambda i: (0, 0)),
                  pl.BlockSpec((tm, tm), lambda i: (0, 0))],
        out_specs=[pl.BlockSpec((tm, LANES), lambda i: (i, 0))] * 3,
        out_shape=[out, out, out],
        scratch_shapes=[pltpu.VMEM((SUBLANES, LANES), F32)],
        compiler_params=_params("arbitrary"),
        name="gates",
    )(x, w_small, bias_a, bias_b, tri)


def _mlstm_kernel(q_ref, k_ref, v_ref, o_ref, ga_ref, gb_ref, arow_ref, c0_ref, n0_ref, m0_ref, g_ref,
                  h_ref, c_ref, n_ref, m_ref, ct_sc, n_sc, a_sc, *, n_heads):
    L = q_ref.shape[0]
    dk = HEAD_DIM

    def init():
        ct_sc[...] = c0_ref[...]
        n_sc[...] = jnp.broadcast_to(n0_ref[...], n_sc.shape)
        a_sc[...] = jnp.broadcast_to(m0_ref[...], a_sc.shape)

    def main():
        lane = lax.broadcasted_iota(jnp.int32, (L, LANES), 1)
        ti = lax.broadcasted_iota(jnp.int32, (L, L), 0)
        si = lax.broadcasted_iota(jnp.int32, (L, L), 1)
        causal = si <= ti
        ga = ga_ref[...]
        gb = gb_ref[...]
        for hd in range(n_heads):
            head(hd, lane, causal, ga, gb)

    def head(hd, lane, causal, ga, gb):
        hs = slice(hd * dk, (hd + 1) * dk)
        q = q_ref[:, hs]
        k = k_ref[:, hs]
        v = v_ref[:, hs]
        sel = lane == hd
        a_col = jnp.sum(jnp.where(sel, ga, 0.0), axis=1, keepdims=True)
        b_col = jnp.sum(jnp.where(sel, gb, 0.0), axis=1, keepdims=True)
        a_row = arow_ref[hd]
        a_prev = a_sc[hd, 0:1, 0:1]
        ct = ct_sc[hd]
        n_row = n_sc[hd, 0:1, :]
        mm = jnp.where(causal, a_row, NEG_INF)
        a_t = jnp.maximum(jnp.max(mm, axis=1, keepdims=True), a_prev)
        s = _dot_nt(q, k) * (dk ** -0.5) * jnp.exp(mm - a_t)
        inter = jnp.exp(a_prev - a_t)
        num = _dot(s.astype(BF16), v) + inter * _dot(q, ct.astype(BF16))
        den = (jnp.sum(s, axis=1, keepdims=True)
               + inter * jnp.sum(q.astype(F32) * n_row, axis=1, keepdims=True))
        h = num / jnp.maximum(jnp.abs(den), jnp.exp(-(b_col + a_t)))
        hm = _sigmoid(o_ref[:, hs].astype(F32)) * h
        hm = hm * lax.rsqrt(jnp.mean(hm * hm, axis=1, keepdims=True) + NORM_EPS) * g_ref[:, hs]
        h_ref[:, hs] = hm.astype(h_ref.dtype)

        a_end = jnp.maximum(jnp.max(a_row, axis=1, keepdims=True), a_prev)
        e_col = jnp.exp(a_col - a_end) * (dk ** -0.5)
        decay = jnp.exp(a_prev - a_end)
        ke = k.astype(F32) * e_col
        ct_new = decay * ct + _dot_tn(ke.astype(BF16), v)
        n_new = decay * n_row + jnp.sum(ke, axis=0, keepdims=True)
        ct_sc[hd] = ct_new
        n_sc[hd] = jnp.broadcast_to(n_new, n_sc.shape[1:])
        a_sc[hd] = jnp.broadcast_to(a_end, a_sc.shape[1:])
        c_ref[hd] = ct_new
        n_ref[hd] = n_new
        m_ref[hd] = jnp.broadcast_to(b_col[L - 1:L, :] + a_end, m_ref.shape[1:])

    return init, main


def _mlstm(zm, ga, gb, a_rows, c0t, n0, m0, g_norm, batch, n_heads, seq):
    L = CHUNK
    nc = seq // L
    n = batch * seq
    dh = HEAD_DIM
    hw = n_heads * dh

    def col(off):
        return pl.BlockSpec((L, hw), lambda b, c: (b * nc + c, off))

    def rows():
        return pl.BlockSpec((L, LANES), lambda b, c: (b * nc + c, 0))

    def per_b(r, w):
        return pl.BlockSpec((None, n_heads, r, w), lambda b, c: (b, 0, 0, 0))

    return dict(
        body=functools.partial(_mlstm_kernel, n_heads=n_heads),
        in_specs=[col(0), col(1), col(2), col(3), rows(), rows(),
                  pl.BlockSpec((None, n_heads, 1, L), lambda b, c: (b, 0, 0, c)),
                  per_b(dh, dh), per_b(1, dh), per_b(1, LANES),
                  pl.BlockSpec((1, hw), lambda b, c: (0, 0))],
        out_specs=[pl.BlockSpec((L, hw), lambda b, c: (b * nc + c, 0)),
                   per_b(dh, dh), per_b(1, dh), per_b(1, LANES)],
        out_shape=[jax.ShapeDtypeStruct((n, hw), BF16),
                   jax.ShapeDtypeStruct((batch, n_heads, dh, dh), F32),
                   jax.ShapeDtypeStruct((batch, n_heads, 1, dh), F32),
                   jax.ShapeDtypeStruct((batch, n_heads, 1, LANES), F32)],
        scratch_shapes=[pltpu.VMEM((n_heads, dh, dh), F32), pltpu.VMEM((n_heads, SUBLANES, dh), F32),
                        pltpu.VMEM((n_heads, SUBLANES, LANES), F32)],
        args=(zm, zm, zm, zm, ga, gb, a_rows, c0t, n0, m0, g_norm))


def _recurrent_mixers(parts, batch, seq):
    n_in = [len(p["in_specs"]) for p in parts]
    n_out = [len(p["out_specs"]) for p in parts]
    n_scr = [len(p["scratch_shapes"]) for p in parts]

    def kernel(*refs):
        ins, outs, scrs = refs[:sum(n_in)], refs[sum(n_in):sum(n_in) + sum(n_out)], refs[sum(n_in) + sum(n_out):]
        i = o = s = 0
        stages = []
        for p, ni, no, ns in zip(parts, n_in, n_out, n_scr):
            stages.append(p["body"](*ins[i:i + ni], *outs[o:o + no], *scrs[s:s + ns]))
            i, o, s = i + ni, o + no, s + ns

        @pl.when(pl.program_id(1) == 0)
        def _():
            for init, _ in stages:
                init()

        for _, main in stages:
            main()

    res = pl.pallas_call(
        kernel,
        grid=(batch, seq // CHUNK),
        in_specs=[sp for p in parts for sp in p["in_specs"]],
        out_specs=[sp for p in parts for sp in p["out_specs"]],
        out_shape=[sh for p in parts for sh in p["out_shape"]],
        scratch_shapes=[sc for p in parts for sc in p["scratch_shapes"]],
        compiler_params=_params("parallel", "arbitrary"),
        name="recurrent_mixers",
    )(*[a for p in parts for a in p["args"]])
    out, o = [], 0
    for no in n_out:
        out.append(res[o:o + no])
        o += no
    return out


def _gla_levels(L):
    levels = []
    w = L // 2
    while w >= 1:
        levels.append(w)
        w //= 2
    return levels


def _gla_consts(L):
    t = np.arange(L)
    row, colj = t[:, None], t[None, :]
    mats = [(colj <= row), (colj > row)]
    masks = []
    for w in _gla_levels(L):
        mid = (t // (2 * w)) * 2 * w + w
        right = t >= mid
        mr = right[:, None] & (colj >= mid[:, None]) & (colj <= row)
        ml = (~right)[:, None] & (colj > row) & (colj < mid[:, None])
        mats.append(mr | ml)
        same = (t[:, None] // (2 * w)) == (t[None, :] // (2 * w))
        masks.append(same & right[:, None] & (~right)[None, :])
    masks.append(row == colj)
    m_all = np.concatenate([m.astype(np.float32) for m in mats], axis=0)
    return m_all, np.stack([m.astype(np.float32) for m in masks])


def _gla_kernel(q_ref, k_ref, v_ref, r_ref, ga_ref, wa_ref, ba_ref, mall_ref, mask_ref, s0_ref, g_ref,
                h_ref, s_ref, st_sc, *, n_valid):
    L = q_ref.shape[0]
    dk = LANES // 2
    dv = HEAD_DIM
    n_lev = mask_ref.shape[0] - 1
    n_pair = st_sc.shape[0]

    def init():
        st_sc[...] = s0_ref[...]

    def main():
        ga = ga_ref[...].astype(BF16)
        mall = mall_ref[...]
        lane = lax.broadcasted_iota(jnp.int32, (1, LANES), 1)
        lm = [(lane < dk).astype(F32), (lane >= dk).astype(F32)]
        for p in range(n_pair):
            pair(p, ga, mall, lm)

    def pair(p, ga, mall, lm):
        ps = slice(p * LANES, (p + 1) * LANES)
        q2 = q_ref[:, ps].astype(F32) * (dk ** -0.5)
        k2 = k_ref[:, ps].astype(F32)
        la = _log_sigmoid(_dot(ga, wa_ref[p]) + ba_ref[p]) * (1.0 / GLA_TAU)
        if n_valid < L:
            valid = lax.broadcasted_iota(jnp.int32, (L, LANES), 0) < n_valid
            la = jnp.where(valid, la, 0.0)
            k2 = jnp.where(valid, k2, 0.0)
        la_hi = la.astype(BF16)
        la_mid = (la - la_hi.astype(F32)).astype(BF16)
        e = jnp.exp(_dot(mall, la_hi) + _dot(mall, la_mid))

        q_in = q2 * e[0:L]
        k_end = (k2 * e[L:2 * L]).astype(BF16)
        decay = e[L - 1:L]
        kq = [(q2 * e[(2 + i) * L:(3 + i) * L], (k2 * e[(2 + i) * L:(3 + i) * L]).astype(BF16))
              for i in range(n_lev)]
        k2b = k2.astype(BF16)
        for hh in range(2):
            hs = slice((2 * p + hh) * dv, (2 * p + hh + 1) * dv)
            v = v_ref[:, hs]
            st = st_sc[p, hh]
            o = _dot_nt((q_in * lm[hh]).astype(BF16), st.astype(BF16))
            a = mask_ref[n_lev] * _dot_nt((q2 * lm[hh]).astype(BF16), k2b)
            for i in range(n_lev):
                qh, kh = kq[i]
                a = a + mask_ref[i] * _dot_nt((qh * lm[hh]).astype(BF16), kh)
            o = o + _dot(a.astype(BF16), v)
            st_new = decay * st + _dot_tn(v, k_end)
            st_sc[p, hh] = st_new
            s_ref[p, hh] = st_new
            o = o * lax.rsqrt(jnp.mean(o * o, axis=1, keepdims=True) + NORM_EPS) * g_ref[:, hs]
            r = r_ref[:, hs].astype(F32)
            h_ref[:, hs] = (r * _sigmoid(r) * o).astype(h_ref.dtype)

    return init, main


def _gla(zg, ga, wa, ba, s0t, g_norm, batch, n_heads, seq, n_valid):
    L = CHUNK
    nc = seq // L
    n = batch * seq
    npair = n_heads // 2
    dv = HEAD_DIM
    m_all, masks = _gla_consts(L)
    m_all = jnp.asarray(m_all, BF16)
    masks = jnp.asarray(masks, F32)
    qkw = npair * LANES
    state = pl.BlockSpec((None, npair, 2, dv, LANES), lambda b, c: (b, 0, 0, 0, 0))

    return dict(
        body=functools.partial(_gla_kernel, n_valid=n_valid),
        in_specs=[pl.BlockSpec((L, qkw), lambda b, c: (b * nc + c, 0)),
                  pl.BlockSpec((L, qkw), lambda b, c: (b * nc + c, 1)),
                  pl.BlockSpec((L, 2 * qkw), lambda b, c: (b * nc + c, 1)),
                  pl.BlockSpec((L, 2 * qkw), lambda b, c: (b * nc + c, 2)),
                  pl.BlockSpec((L, LANES), lambda b, c: (b * nc + c, 0)),
                  pl.BlockSpec(wa.shape, lambda b, c: (0, 0, 0)),
                  pl.BlockSpec(ba.shape, lambda b, c: (0, 0, 0)),
                  pl.BlockSpec(m_all.shape, lambda b, c: (0, 0)),
                  pl.BlockSpec(masks.shape, lambda b, c: (0, 0, 0)),
                  state,
                  pl.BlockSpec((1, n_heads * dv), lambda b, c: (0, 0))],
        out_specs=[pl.BlockSpec((L, n_heads * dv), lambda b, c: (b * nc + c, 0)), state],
        out_shape=[jax.ShapeDtypeStruct((n, n_heads * dv), BF16),
                   jax.ShapeDtypeStruct((batch, npair, 2, dv, LANES), F32)],
        scratch_shapes=[pltpu.VMEM((npair, 2, dv, LANES), F32)],
        args=(zg, zg, zg, zg, ga, wa, ba, m_all, masks, s0t, g_norm))


def _fox_prompt_kernel(qi_ref, ki_ref, q_ref, k_ref, v_ref, gb_ref, crow_ref, o_ref, m_sc, acc_sc, ccol_sc,
                       *, n_heads, lane0):
    g = pl.program_id(0)
    p = pl.program_id(1)
    qi = qi_ref[p]
    ki = ki_ref[p]
    tq = q_ref.shape[0]
    tk = k_ref.shape[0]
    dh = HEAD_DIM
    hpg = q_ref.shape[1] // dh
    groups_per_batch = n_heads // hpg

    @pl.when(ki == 0)
    def _():
        m_sc[...] = jnp.full_like(m_sc, NEG_INF)
        acc_sc[...] = jnp.zeros_like(acc_sc)
        lane = lax.broadcasted_iota(jnp.int32, (tq, LANES), 1)
        gb = gb_ref[...]
        for hh in range(hpg):
            hd = (g % groups_per_batch) * hpg + hh
            ccol_sc[hh] = LOG2E * jnp.sum(jnp.where(lane == lane0 + hd, gb, 0.0), axis=1, keepdims=True)

    def step(diagonal):
        for hh in range(hpg):
            hs = slice(hh * dh, (hh + 1) * dh)
            u = _dot_nt(q_ref[:, hs], k_ref[:, hs]) - LOG2E * crow_ref[hh]
            if diagonal:
                ti = lax.broadcasted_iota(jnp.int32, (tq, tk), 0)
                si = lax.broadcasted_iota(jnp.int32, (tq, tk), 1)
                u = jnp.where(si <= ti, u, NEG_INF)
            c2 = ccol_sc[hh]
            m_old = m_sc[hh]
            m_new = jnp.maximum(m_old, jnp.max(u, axis=1, keepdims=True) + c2)
            pr = jnp.exp2(u - (m_new - c2))
            v1 = jnp.concatenate([v_ref[:, hs], jnp.ones((tk, LANES), BF16)], axis=1)
            acc_sc[hh] = jnp.exp2(m_old - m_new) * acc_sc[hh] + _dot(pr.astype(BF16), v1)
            m_sc[hh] = m_new

    @pl.when(ki < qi)
    def _():
        step(False)

    @pl.when(ki == qi)
    def _():
        step(True)
        for hh in range(hpg):
            acc = acc_sc[hh]
            o_ref[:, hh * dh:(hh + 1) * dh] = (acc[:, :dh] / acc[:, dh:]).astype(o_ref.dtype)


def _fox_prompt(zq, zk, zv, gb, c_rows, batch, n_heads, seq, tile, lane0):
    n = batch * seq
    nq = seq // tile
    dh = HEAD_DIM
    hpg = ATTN_HEADS_PER_STEP
    gpb = n_heads // hpg
    pairs = [(a, b) for a in range(nq) for b in range(a + 1)]
    qi = jnp.asarray([a for a, _ in pairs], jnp.int32)
    ki = jnp.asarray([b for _, b in pairs], jnp.int32)
    c_rows = c_rows.reshape(batch * gpb, hpg, 1, seq)

    def qrow(g, p, qi, ki):
        return (g // gpb) * nq + qi[p]

    def krow(g, p, qi, ki):
        return (g // gpb) * nq + ki[p]

    grid_spec = pltpu.PrefetchScalarGridSpec(
        num_scalar_prefetch=2,
        grid=(batch * gpb, len(pairs)),
        in_specs=[pl.BlockSpec((tile, hpg * dh), lambda g, p, qi, ki: (qrow(g, p, qi, ki), g % gpb)),
                  pl.BlockSpec((tile, hpg * dh), lambda g, p, qi, ki: (krow(g, p, qi, ki), g % gpb)),
                  pl.BlockSpec((tile, hpg * dh), lambda g, p, qi, ki: (krow(g, p, qi, ki), g % gpb)),
                  pl.BlockSpec((tile, LANES), lambda g, p, qi, ki: (qrow(g, p, qi, ki), 0)),
                  pl.BlockSpec((None, hpg, 1, tile), lambda g, p, qi, ki: (g, 0, 0, ki[p]))],
        out_specs=pl.BlockSpec((tile, hpg * dh), lambda g, p, qi, ki: (qrow(g, p, qi, ki), g % gpb)),
        scratch_shapes=[pltpu.VMEM((hpg, tile, 1), F32), pltpu.VMEM((hpg, tile, dh + LANES), F32),
                        pltpu.VMEM((hpg, tile, 1), F32)],
    )
    return pl.pallas_call(
        functools.partial(_fox_prompt_kernel, n_heads=n_heads, lane0=lane0),
        grid_spec=grid_spec,
        out_shape=jax.ShapeDtypeStruct((n, n_heads * dh), BF16),
        compiler_params=_params("parallel", "arbitrary"),
        name="fox_prompt",
    )(qi, ki, zq, zk, zv, gb, c_rows)


def _fox_sample_kernel(pt_ref, q_ref, cq_ref, kn_ref, vn_ref, bn_ref, *rest, n_heads, n_new, group):
    k_refs, v_refs, lf_refs = rest[:group], rest[group:2 * group], rest[2 * group:3 * group]
    o_ref, m_sc, l_sc, acc_sc, carry_sc = rest[3 * group:]
    j = pl.program_id(1)
    nr = q_ref.shape[0]
    rows_pp = k_refs[0].shape[0]
    d = q_ref.shape[1]
    n_tiles = rows_pp // LANES
    scale = d ** -0.5

    @pl.when(j == 0)
    def _():
        m_sc[...] = jnp.full_like(m_sc, NEG_INF)
        l_sc[...] = jnp.zeros_like(l_sc)
        acc_sc[...] = jnp.zeros_like(acc_sc)
        carry_sc[...] = jnp.zeros_like(carry_sc)

    lane8 = lax.broadcasted_iota(jnp.int32, (SUBLANES, LANES), 1)
    row8 = lax.broadcasted_iota(jnp.int32, (SUBLANES, LANES), 0)

    def page_suffix(x):
        y = x
        z = x
        sh = n_heads
        while sh < LANES:
            y = y + jnp.where(lane8 + sh < LANES, pltpu.roll(y, LANES - sh, 1), 0.0)
            z = z + pltpu.roll(z, sh, 1)
            sh *= 2
        w = z
        sh = 1
        while sh < SUBLANES:
            w = w + jnp.where(row8 + sh < SUBLANES, pltpu.roll(w, SUBLANES - sh, 0), 0.0)
            sh *= 2
        return y - x + (w - z), jnp.broadcast_to(w[0:1, :], x.shape)

    q = q_ref[...]
    cq = cq_ref[...]
    rowi = lax.broadcasted_iota(jnp.int32, (nr, LANES), 0)
    lanei = lax.broadcasted_iota(jnp.int32, (nr, LANES), 1)
    head_ok = (rowi // n_new) == (lanei % n_heads)

    def update(tile_groups, vbs):
        flat = [t for ts in tile_groups for t in ts]
        m_old = m_sc[...]
        mx = flat[0]
        for t in flat[1:]:
            mx = jnp.maximum(mx, t)
        m_new = jnp.maximum(m_old, jnp.max(mx, axis=1, keepdims=True))
        alpha = jnp.exp(m_old - m_new)
        tot = None
        acc = alpha * acc_sc[...]
        for ts, vb in zip(tile_groups, vbs):
            ps = [jnp.exp(t - m_new) for t in ts]
            for t in ps:
                tot = t if tot is None else tot + t
            pcat = ps[0] if len(ps) == 1 else jnp.concatenate(ps, axis=1)
            acc = acc + _dot(pcat.astype(BF16), vb)
        l_sc[...] = alpha * l_sc[...] + jnp.sum(tot, axis=1, keepdims=True)
        acc_sc[...] = acc
        m_sc[...] = m_new

    carry = carry_sc[...]
    tile_groups = []
    for g in range(group):
        within, total = page_suffix(lf_refs[g][...])
        suffix = within + carry
        carry = carry + total
        s = _dot_nt(q, k_refs[g][...].astype(BF16)) * scale
        tile_groups.append([jnp.where(head_ok, s[:, r * LANES:(r + 1) * LANES] + cq + suffix[r:r + 1, :], NEG_INF)
                            for r in range(n_tiles)])
    carry_sc[...] = carry
    update(tile_groups, [v_refs[g][...].astype(BF16) for g in range(group)])

    @pl.when(j == pl.num_programs(1) - 1)
    def _():
        s_self = _dot_nt(q, kn_ref[...]) * scale + cq + bn_ref[...]
        ok = head_ok & (lanei < n_new * n_heads) & ((lanei // n_heads) <= (rowi % n_new))
        update([[jnp.where(ok, s_self, NEG_INF)]], [vn_ref[...]])
        o_ref[...] = acc_sc[...] / l_sc[...]


def _fox_sample(page_table, q, cq, kn, vn, bn, cache_k, cache_v, cache_lf, layer, n_heads, n_new):
    nb, n_pages = page_table.shape
    nr = q.shape[1]
    rows_pp = cache_k.shape[2]
    d = q.shape[2]
    group = max(g for g in (1, 2, 4, 8, 16) if n_pages % g == 0)

    def page_spec(rows, width, g):
        return pl.BlockSpec((None, None, rows, width),
                            lambda b, j, pt: (layer, pt[b, n_pages - 1 - (j * group + g)], 0, 0))

    per_b = lambda rows, width: pl.BlockSpec((None, rows, width), lambda b, j, pt: (b, 0, 0))
    grid_spec = pltpu.PrefetchScalarGridSpec(
        num_scalar_prefetch=1,
        grid=(nb, n_pages // group),
        in_specs=([per_b(nr, d), per_b(nr, LANES), per_b(LANES, d), per_b(LANES, d), per_b(1, LANES)]
                  + [page_spec(rows_pp, d, g) for g in range(group)]
                  + [page_spec(rows_pp, d, g) for g in range(group)]
                  + [page_spec(SUBLANES, LANES, g) for g in range(group)]),
        out_specs=per_b(nr, d),
        scratch_shapes=[pltpu.VMEM((nr, 1), F32), pltpu.VMEM((nr, 1), F32), pltpu.VMEM((nr, d), F32),
                        pltpu.VMEM((SUBLANES, LANES), F32)],
    )
    return pl.pallas_call(
        functools.partial(_fox_sample_kernel, n_heads=n_heads, n_new=n_new, group=group),
        grid_spec=grid_spec,
        out_shape=jax.ShapeDtypeStruct((nb, nr, d), F32),
        compiler_params=_params("parallel", "arbitrary"),
        name="fox_sample",
    )(page_table, q, cq, kn, vn, bn, *([cache_k] * group), *([cache_v] * group), *([cache_lf] * group))


def _layer_norm(r, g, b):
    mu = jnp.mean(r, axis=1, keepdims=True)
    xc = r - mu
    var = jnp.mean(xc * xc, axis=1, keepdims=True)
    return xc * lax.rsqrt(var + LN_EPS) * g + b


def _outproj_kernel(hm_ref, hg_ref, hf_ref, wm_ref, wg_ref, wf_ref, x_ref, g_ref, b_ref, h_ref, hb_ref, *, alpha):
    mix = _dot(hm_ref[...], wm_ref[...]) + _dot(hg_ref[...], wg_ref[...]) + _dot(hf_ref[...], wf_ref[...])
    h = _layer_norm(alpha * x_ref[...] + mix, g_ref[...], b_ref[...])
    h_ref[...] = h
    hb_ref[...] = h.astype(BF16)


def _outproj_ln(hm, hg, hf, w_out, x, g, b, alpha, tm):
    n, d = x.shape
    tm = min(tm, n)
    wm, wg, wf = hm.shape[1], hg.shape[1], hf.shape[1]
    assert wm == wg and wf == wm + wg and n % tm == 0
    const = pl.Buffered(1)
    return pl.pallas_call(
        functools.partial(_outproj_kernel, alpha=alpha),
        grid=(n // tm,),
        in_specs=[pl.BlockSpec((tm, wm), lambda i: (i, 0)),
                  pl.BlockSpec((tm, wg), lambda i: (i, 0)),
                  pl.BlockSpec((tm, wf), lambda i: (i, 0)),
                  pl.BlockSpec((wm, d), lambda i: (0, 0), pipeline_mode=const),
                  pl.BlockSpec((wg, d), lambda i: (1, 0), pipeline_mode=const),
                  pl.BlockSpec((wf, d), lambda i: (1, 0), pipeline_mode=const),
                  pl.BlockSpec((tm, d), lambda i: (i, 0)),
                  pl.BlockSpec((1, d), lambda i: (0, 0)),
                  pl.BlockSpec((1, d), lambda i: (0, 0))],
        out_specs=[pl.BlockSpec((tm, d), lambda i: (i, 0))] * 2,
        out_shape=[jax.ShapeDtypeStruct((n, d), F32), jax.ShapeDtypeStruct((n, d), BF16)],
        compiler_params=_params("parallel"),
        name="outproj_ln",
    )(hm, hg, hf, w_out, w_out, w_out, x, g, b)


def _ffn_up_kernel(x_ref, wg_ref, wv_ref, cwg_ref, cwv_ref, cbg_ref, cbv_ref, hg0_ref, hv0_ref,
                   o_ref, tg_ref, tv_ref, halo_g, halo_v, win_g, win_v, *, tiles_per_seq, shift):
    i = pl.program_id(0)
    j = pl.program_id(1)
    tm = x_ref.shape[0]
    hr = hg0_ref.shape[0]
    tn = o_ref.shape[1]

    @pl.when(i % tiles_per_seq == 0)
    def _():
        halo_g[j] = hg0_ref[...]
        halo_v[j] = hv0_ref[...]

    rc = min(FFN_ROW_CHUNK, tm)

    rb = min(FFN_EPILOGUE_ROWS, rc)

    def conv(r0, cs, cw_ref, cb_ref, win):
        cw = cw_ref[:, cs]
        return (cb_ref[:, cs] + cw[0:1] * win[pl.ds(hr + r0 - 2 * shift, rb), cs]
                + cw[1:2] * win[pl.ds(hr + r0 - shift, rb), cs] + cw[2:3] * win[pl.ds(hr + r0, rb), cs])

    win_g[0:hr, :] = halo_g[j]
    win_v[0:hr, :] = halo_v[j]
    for c0 in range(0, tn, FFN_COL_SLAB):
        cs = slice(c0, c0 + FFN_COL_SLAB)
        for r0 in range(0, tm, rc):
            rs = slice(r0, r0 + rc)
            ug = _dot(x_ref[rs, :], wg_ref[:, cs])
            uv = _dot(x_ref[rs, :], wv_ref[:, cs])
            win_g[hr + r0:hr + r0 + rc, cs] = ug
            win_v[hr + r0:hr + r0 + rc, cs] = uv
            for r1 in range(r0, r0 + rc, rb):
                yg = conv(r1, cs, cwg_ref, cbg_ref, win_g)
                yv = conv(r1, cs, cwv_ref, cbv_ref, win_v)
                o_ref[r1:r1 + rb, cs] = (yg * _sigmoid(yg) * yv).astype(o_ref.dtype)
    tail_g = win_g[tm:tm + hr, :]
    tail_v = win_v[tm:tm + hr, :]
    tg_ref[...] = tail_g
    tv_ref[...] = tail_v
    halo_g[j] = tail_g
    halo_v[j] = tail_v


def _ffn_up(x, w_up, conv_w, conv_b, halo0, fp, tm, tn, tiles_per_seq, shift):
    n, d = x.shape
    tm = min(tm, n)
    nj = fp // tn
    ni = n // tm
    hr = halo0.shape[0] // (ni // tiles_per_seq)
    assert n % tm == 0 and fp % tn == 0 and hr >= 2 * shift
    seq = lambda i: i // tiles_per_seq
    return pl.pallas_call(
        functools.partial(_ffn_up_kernel, tiles_per_seq=tiles_per_seq, shift=shift),
        grid=(ni, nj),
        in_specs=[pl.BlockSpec((tm, d), lambda i, j: (i, 0)),
                  pl.BlockSpec((d, tn), lambda i, j: (0, j)),
                  pl.BlockSpec((d, tn), lambda i, j: (0, nj + j)),
                  pl.BlockSpec((CONV_W, tn), lambda i, j: (0, j)),
                  pl.BlockSpec((CONV_W, tn), lambda i, j: (0, nj + j)),
                  pl.BlockSpec((1, tn), lambda i, j: (0, j)),
                  pl.BlockSpec((1, tn), lambda i, j: (0, nj + j)),
                  pl.BlockSpec((hr, tn), lambda i, j: (seq(i), j)),
                  pl.BlockSpec((hr, tn), lambda i, j: (seq(i), nj + j))],
        out_specs=[pl.BlockSpec((tm, tn), lambda i, j: (i, j)),
                   pl.BlockSpec((hr, tn), lambda i, j: (i, j)),
                   pl.BlockSpec((hr, tn), lambda i, j: (i, j))],
        out_shape=[jax.ShapeDtypeStruct((n, fp), BF16),
                   jax.ShapeDtypeStruct((ni * hr, fp), F32),
                   jax.ShapeDtypeStruct((ni * hr, fp), F32)],
        scratch_shapes=[pltpu.VMEM((nj, hr, tn), F32), pltpu.VMEM((nj, hr, tn), F32),
                        pltpu.VMEM((hr + tm, tn), F32), pltpu.VMEM((hr + tm, tn), F32)],
        compiler_params=_params("arbitrary", "arbitrary"),
        name="ffn_up",
    )(x, w_up, w_up, conv_w, conv_w, conv_b, conv_b, halo0, halo0)


def _ffn_down_kernel(a_ref, w_ref, h_ref, g_ref, b_ref, x_ref, xb_ref, *, alpha):
    x = _layer_norm(alpha * h_ref[...] + _dot(a_ref[...], w_ref[...]), g_ref[...], b_ref[...])
    x_ref[...] = x
    xb_ref[...] = x.astype(BF16)


def _ffn_down_ln(a, w_down, h, g, b, alpha, tm):
    n, fp = a.shape
    d = w_down.shape[1]
    tm = min(tm, n)
    assert n % tm == 0
    return pl.pallas_call(
        functools.partial(_ffn_down_kernel, alpha=alpha),
        grid=(n // tm,),
        in_specs=[pl.BlockSpec((tm, fp), lambda i: (i, 0)),
                  pl.BlockSpec((fp, d), lambda i: (0, 0), pipeline_mode=pl.Buffered(1)),
                  pl.BlockSpec((tm, d), lambda i: (i, 0)),
                  pl.BlockSpec((1, d), lambda i: (0, 0)),
                  pl.BlockSpec((1, d), lambda i: (0, 0))],
        out_specs=[pl.BlockSpec((tm, d), lambda i: (i, 0))] * 2,
        out_shape=[jax.ShapeDtypeStruct((n, d), F32), jax.ShapeDtypeStruct((n, d), BF16)],
        compiler_params=_params("parallel"),
        name="ffn_down_ln",
    )(a, w_down, h, g, b)


def _prep_layer(l, dims, w_in, b_m_ig, b_m_fg, w_g_alpha_up, b_g_alpha, b_f, g_m_norm, g_g_norm, w_out,
                ln1_g, ln1_b, w_up, conv_w, conv_b, w_down, ln2_g, ln2_b):
    d, hm, hg, hf, f, fp = dims
    dh = HEAD_DIM
    dkg = dh // 2
    sizes = [hm * dh] * 4 + [hm, hm] + [hg * dkg] * 2 + [hg * dh] * 2 + [GLA_RANK] + [hf * dh] * 3 + [hf]
    offs = np.concatenate([[0], np.cumsum(sizes)]).tolist()
    (o_mq, o_mk, o_mv, o_mo, o_mi, o_mf, o_gq, o_gk, o_gv, o_gr, o_ga, o_fq, o_fk, o_fv, o_ff, _) = offs
    w = w_in[l]
    col = lambda a, b: w[:, a:b]
    zeros = lambda c: jnp.zeros((d, c), F32)
    blk_a = jnp.concatenate([col(o_mi, o_mi + hm), zeros(8 - hm), col(o_ga, o_ga + GLA_RANK),
                             zeros(LANES - 8 - GLA_RANK)], axis=1)
    blk_b = jnp.concatenate([col(o_mf, o_mf + hm), zeros(8 - hm), col(o_ff, o_ff + hf),
                             zeros(LANES - 8 - hf)], axis=1)
    pad1 = lambda v, lo, total: jnp.pad(v, (lo, total - lo - v.shape[0]))[None, :]
    wa = jnp.zeros((LANES, hg * dkg), F32).at[8:8 + GLA_RANK].set(w_g_alpha_up[l])
    npair = hg // 2
    pad_f = lambda a: jnp.pad(a, [(0, 0)] * (a.ndim - 1) + [(0, fp - f)])
    return dict(
        w_m=col(o_mq, o_mi).astype(BF16),
        w_g=col(o_gq, o_ga).astype(BF16),
        w_fq=col(o_fq, o_fk).astype(BF16),
        w_fk=col(o_fk, o_fv).astype(BF16),
        w_fv=col(o_fv, o_ff).astype(BF16),
        w_small=jnp.concatenate([blk_a, blk_b], axis=1).astype(BF16),
        bias_a=pad1(b_m_ig[l], 0, LANES),
        bias_b=pad1(b_m_fg[l], 0, LANES) + pad1(b_f[l], 8, LANES),
        wa=wa.reshape(LANES, npair, LANES).transpose(1, 0, 2).astype(BF16),
        ba=b_g_alpha[l].reshape(npair, 1, LANES),
        g_m=g_m_norm[l][None, :], g_g=g_g_norm[l][None, :],
        w_out=w_out[l].astype(BF16),
        ln1_g=ln1_g[l][None, :], ln1_b=ln1_b[l][None, :], ln2_g=ln2_g[l][None, :], ln2_b=ln2_b[l][None, :],
        w_up=_cast_pad_cols(w_up, l, 2, fp, 256),
        conv_w=jnp.concatenate([pad_f(conv_w[l][:, :f]), pad_f(conv_w[l][:, f:])], axis=1),
        conv_b=jnp.concatenate([pad_f(conv_b[l][:f]), pad_f(conv_b[l][f:])])[None, :],
        w_down=_cast_pad_rows(w_down, l, fp, 512),
    )


def _pad_conv_state(s, f, fp):
    pad = [(0, 0)] * (s.ndim - 1) + [(0, fp - f)]
    return jnp.concatenate([jnp.pad(s[..., :f], pad), jnp.pad(s[..., f:], pad)], axis=-1)


def _mixer_rows(x_bf, p, tm, q_scale=1.0):
    zm, = _matmul(x_bf, p["w_m"], [BF16], tm, 512, "proj_m")
    zg, = _matmul(x_bf, p["w_g"], [BF16], tm, 512, "proj_g")
    zq, = _matmul(x_bf, p["w_fq"], [BF16], tm, 512, "proj_fq", scale=q_scale)
    zk, zkb = _matmul(x_bf, p["w_fk"], [F32, BF16], tm, 512, "proj_fk")
    zv, zvb = _matmul(x_bf, p["w_fv"], [F32, BF16], tm, 512, "proj_fv")
    return zm, zg, zq, zk, zkb, zv, zvb


def kernel(x_prompt, x_sample, state_mlstm_c, state_mlstm_n, state_mlstm_m, state_gla, state_ffn_conv,
           cache_k, cache_v, cache_logf, page_table,
           w_in, b_m_ig, b_m_fg, w_g_alpha_up, b_g_alpha, b_f, g_m_norm, g_g_norm, w_out,
           ln1_g, ln1_b, w_up, conv_w, conv_b, w_down, ln2_g, ln2_b):
    bp, seq, d = x_prompt.shape
    db, ns, _ = x_sample.shape
    depth = w_in.shape[0]
    hm = b_m_ig.shape[1]
    hf = b_f.shape[1]
    dh = HEAD_DIM
    hg = g_g_norm.shape[1] // dh
    dkg = dh // 2
    f = w_down.shape[1]
    fp = -(-f // 512) * 512
    alpha = (2.0 * depth) ** 0.25
    n_pool, page = cache_k.shape[1], cache_k.shape[2]
    assert hm <= 8 and hf <= 8 and hg % 2 == 0 and seq % 512 == 0 and ns <= CHUNK
    assert page * hf == SUBLANES * LANES and ns * hf <= LANES
    dims = (d, hm, hg, hf, f, fp)
    L = CHUNK
    n_p = bp * seq

    tri_p = jnp.asarray(np.tril(np.ones((GATE_TILE, GATE_TILE), np.float32)), BF16)
    gs_rows = LANES
    r = np.arange(gs_rows)
    tri_s_np = ((r[:, None] % db == r[None, :] % db) & (r[None, :] // db <= r[:, None] // db)
                & (r[:, None] < ns * db) & (r[None, :] < ns * db))
    tri_s = jnp.asarray(tri_s_np.astype(np.float32), BF16)

    xp = x_prompt.astype(F32).reshape(n_p, d)
    xp_bf = xp.astype(BF16)
    xs = x_sample.astype(F32).transpose(1, 0, 2).reshape(ns * db, d)
    xs_bf = xs.astype(BF16)

    ck = cache_k.astype(F32).reshape(depth, n_pool, page * hf, dh)
    cv = cache_v.astype(F32).reshape(depth, n_pool, page * hf, dh)
    clf = cache_logf.astype(F32).reshape(depth, n_pool, SUBLANES, LANES)

    def to_padded(z, pad_value=0.0, mode="constant"):
        c = z.shape[-1]
        a = z.reshape(ns, db, c).transpose(1, 0, 2)
        if mode == "edge":
            a = jnp.pad(a, ((0, 0), (0, L - ns), (0, 0)), mode="edge")
        else:
            a = jnp.pad(a, ((0, 0), (0, L - ns), (0, 0)), constant_values=pad_value)
        return a.reshape(db * L, c)

    def from_padded(y):
        c = y.shape[-1]
        return y.reshape(db, L, c)[:, :ns].transpose(1, 0, 2).reshape(ns * db, c)

    def head_rows(g, lane0, nh, batch, t):
        return g[:, lane0:lane0 + nh].reshape(batch, t, nh).transpose(0, 2, 1).reshape(batch * nh, 1, t)

    outs_p = [[] for _ in range(8)]
    outs_s = [[] for _ in range(8)]
    for l in range(depth):
        p = _prep_layer(l, dims, w_in, b_m_ig, b_m_fg, w_g_alpha_up, b_g_alpha, b_f, g_m_norm, g_g_norm, w_out,
                        ln1_g, ln1_b, w_up, conv_w, conv_b, w_down, ln2_g, ln2_b)

        zm, zg, zq, zk, zkb, zv, zvb = _mixer_rows(xp_bf, p, 2048, q_scale=LOG2E * dh ** -0.5)
        ga, gb, gc = _gates(xp_bf, p["w_small"], p["bias_a"], p["bias_b"], tri_p, seq // GATE_TILE, hm)
        zeros = lambda *s: jnp.zeros(s, F32)
        (h_g, st), (h_m, ct, n_m, m_m) = _recurrent_mixers(
            [_gla(zg, ga, p["wa"], p["ba"], zeros(bp, hg // 2, 2, dh, LANES), p["g_g"], bp, hg, seq, L),
             _mlstm(zm, ga, gb, head_rows(ga, 0, hm, bp, seq).reshape(bp, hm, 1, seq),
                    zeros(bp, hm, dh, dh), zeros(bp, hm, 1, dh), zeros(bp, hm, 1, LANES), p["g_m"], bp, hm, seq)],
            bp, seq)
        h_f = _fox_prompt(zq, zkb, zvb, gb, head_rows(gb, 8, hf, bp, seq), bp, hf, seq, ATTN_TILE, 8)
        h, h_bf = _outproj_ln(h_m, h_g, h_f, p["w_out"], xp, p["ln1_g"], p["ln1_b"], alpha, 512)
        tm_f = min(FFN_ROW_TILE, seq)
        act, tg, tv = _ffn_up(h_bf, p["w_up"], p["conv_w"], p["conv_b"], zeros(bp * SUBLANES, 2 * fp),
                              fp, tm_f, 512, seq // tm_f, 1)
        xp, xp_bf = _ffn_down_ln(act, p["w_down"], h, p["ln2_g"], p["ln2_b"], alpha, FFN_DOWN_ROW_TILE)

        def conv_tail_p(t):
            t = t.reshape(bp, seq // tm_f, SUBLANES, fp)[:, -1, SUBLANES - (CONV_W - 1):, :f]
            return t
        outs_p[0].append(ct.reshape(bp, hm, dh, dh).swapaxes(-1, -2))
        outs_p[1].append(n_m.reshape(bp, hm, dh))
        outs_p[2].append(m_m[:, :, 0, 0])
        st_h = st.reshape(bp, hg // 2, 2, dh, 2, dkg)
        st_h = jnp.stack([st_h[:, :, 0, :, 0, :], st_h[:, :, 1, :, 1, :]], axis=2)
        outs_p[3].append(st_h.reshape(bp, hg, dh, dkg).swapaxes(-1, -2))
        outs_p[4].append(jnp.concatenate([conv_tail_p(tg), conv_tail_p(tv)], axis=-1))
        outs_p[5].append(zk.reshape(bp, seq, hf, dh))
        outs_p[6].append(zv.reshape(bp, seq, hf, dh))
        outs_p[7].append(gc[:, 8:8 + hf].reshape(bp, seq, hf))

        n_s = ns * db
        zm, zg, zq, zk, zkb, zv, zvb = _mixer_rows(xs_bf, p, n_s)
        xs_pad = jnp.pad(xs_bf, ((0, gs_rows - n_s), (0, 0)))
        ga, gb, gc = _gates(xs_pad, p["w_small"], p["bias_a"], p["bias_b"], tri_s, 1, hm)
        ga, gb, gc = ga[:n_s], gb[:n_s], gc[:n_s]
        ga_p = to_padded(ga, NEG_INF)
        gb_p = to_padded(gb, mode="edge")
        c0t = state_mlstm_c[l].astype(F32).swapaxes(-1, -2)
        n0 = state_mlstm_n[l].astype(F32).reshape(db, hm, 1, dh)
        m0 = jnp.broadcast_to(state_mlstm_m[l].astype(F32).reshape(db, hm, 1, 1), (db, hm, 1, LANES))
        s0 = state_gla[l].astype(F32).swapaxes(-1, -2).reshape(db, hg // 2, 2, dh, dkg)
        s0t = jnp.stack([jnp.pad(s0[:, :, 0], ((0, 0), (0, 0), (0, 0), (0, dkg))),
                         jnp.pad(s0[:, :, 1], ((0, 0), (0, 0), (0, 0), (dkg, 0)))], axis=2)
        (h_m, ct, n_m, m_m), (h_g, st) = _recurrent_mixers(
            [_mlstm(to_padded(zm), ga_p, gb_p, head_rows(ga_p, 0, hm, db, L).reshape(db, hm, 1, L),
                    c0t, n0, m0, p["g_m"], db, hm, L),
             _gla(to_padded(zg), to_padded(ga), p["wa"], p["ba"], s0t, p["g_g"], db, hg, L, ns)],
            db, L)

        q_s = zq.reshape(ns, db, hf, dh).transpose(1, 2, 0, 3).reshape(db, hf * ns, dh)
        new_rows = lambda z: jnp.pad(z.reshape(ns, db, hf * dh).transpose(1, 0, 2).reshape(db, ns * hf, dh),
                                     ((0, 0), (0, LANES - ns * hf), (0, 0)))
        c_new = gb[:, 8:8 + hf].reshape(ns, db, hf)
        cq = jnp.broadcast_to(c_new.transpose(1, 2, 0).reshape(db, hf * ns, 1), (db, hf * ns, LANES))
        bn = jnp.pad(-c_new.transpose(1, 0, 2).reshape(db, 1, ns * hf), ((0, 0), (0, 0), (0, LANES - ns * hf)))
        o_f = _fox_sample(page_table, q_s, cq, new_rows(zkb), new_rows(zvb), bn, ck, cv, clf, l, hf, ns)
        h_f = o_f.reshape(db, hf, ns, dh).transpose(2, 0, 1, 3).reshape(n_s, hf * dh).astype(BF16)

        h, h_bf = _outproj_ln(from_padded(h_m), from_padded(h_g), h_f, p["w_out"], xs, p["ln1_g"], p["ln1_b"],
                              alpha, n_s)
        halo_s = _pad_conv_state(state_ffn_conv[l].astype(F32).transpose(1, 0, 2).reshape((CONV_W - 1) * db, 2 * f),
                                 f, fp)
        act, tg, tv = _ffn_up(h_bf, p["w_up"], p["conv_w"], p["conv_b"], halo_s, fp, n_s, 512, 1, db)
        xs, xs_bf = _ffn_down_ln(act, p["w_down"], h, p["ln2_g"], p["ln2_b"], alpha, n_s)

        conv_tail_s = lambda t: t[:, :f].reshape(CONV_W - 1, db, f).transpose(1, 0, 2)
        outs_s[0].append(ct.reshape(db, hm, dh, dh).swapaxes(-1, -2))
        outs_s[1].append(n_m.reshape(db, hm, dh))
        outs_s[2].append(m_m[:, :, 0, 0])
        st_h = st.reshape(db, hg // 2, 2, dh, 2, dkg)
        st_h = jnp.stack([st_h[:, :, 0, :, 0, :], st_h[:, :, 1, :, 1, :]], axis=2)
        outs_s[3].append(st_h.reshape(db, hg, dh, dkg).swapaxes(-1, -2))
        outs_s[4].append(jnp.concatenate([conv_tail_s(tg), conv_tail_s(tv)], axis=-1))
        outs_s[5].append(zk.reshape(ns, db, hf, dh).transpose(1, 0, 2, 3))
        outs_s[6].append(zv.reshape(ns, db, hf, dh).transpose(1, 0, 2, 3))
        outs_s[7].append(gc[:, 8:8 + hf].reshape(ns, db, hf).transpose(1, 0, 2))

    y_p = xp.reshape(bp, seq, d)
    y_s = xs.reshape(ns, db, d).transpose(1, 0, 2)
    return (y_p, y_s) + tuple(jnp.stack(a) for a in outs_p) + tuple(jnp.stack(a) for a in outs_s)
```
